```python
import math
import jax, jax.numpy as jnp
from jax import lax
import numpy as np

D_MODEL = 2048
BATCH = 8
SEQ = 2048
DEPTH = 1

MIX_WIDTH = D_MODEL
GDN_WIDTH = MIX_WIDTH // 2
POOL_WIDTH = MIX_WIDTH - GDN_WIDTH
GDN_HEAD_DIM = 128
GDN_HEADS = GDN_WIDTH // GDN_HEAD_DIM
CONV_K = 4
CHUNK = 64
POOL_WINDOWS = (2, 4, 8, 16)
POOL_GROUPS = len(POOL_WINDOWS)
POOL_GROUP_DIM = POOL_WIDTH // POOL_GROUPS
MEM_LEN = 256
XATTN_HEADS = 4
XATTN_HEAD_DIM = D_MODEL // XATTN_HEADS
D_FF = 4 * D_MODEL
IN_COLS = 4 * GDN_WIDTH + 2 * GDN_HEADS + POOL_WIDTH
DEEPNORM_ALPHA = (2.0 * DEPTH) ** 0.25
DEEPNORM_BETA = (8.0 * DEPTH) ** -0.25
LN_EPS = 1e-5
NORM_EPS = 1e-6

kernel_name = "hybrid_gdn_pool_deepnorm_layer"


def layer_norm(x, g, b):
    xf = x.astype(jnp.float32)
    mu = jnp.mean(xf, axis=-1, keepdims=True)
    xc = xf - mu
    var = jnp.mean(xc * xc, axis=-1, keepdims=True)
    y = xc * lax.rsqrt(var + LN_EPS) * g.astype(jnp.float32) + b.astype(jnp.float32)
    return y.astype(x.dtype)


def l2norm(x):
    return x * lax.rsqrt(jnp.sum(x * x, axis=-1, keepdims=True) + NORM_EPS)


def causal_dwconv(x, w):
    c = x.shape[-1]
    return lax.conv_general_dilated(
        x, w.astype(x.dtype)[:, None, :], window_strides=(1,), padding=[(CONV_K - 1, 0)],
        dimension_numbers=("NWC", "WIO", "NWC"), feature_group_count=c)


def chunk_gated_delta_rule(q, k, v, g, beta):
    bsz, t_len, h, dk = q.shape
    dv = v.shape[-1]
    n = t_len // CHUNK

    def to_chunks(u):
        return u.reshape(bsz, n, CHUNK, h, u.shape[-1]).transpose(1, 0, 3, 2, 4)

    q = to_chunks(q * (dk ** -0.5))
    k = to_chunks(k)
    v = to_chunks(v)
    g = g.reshape(bsz, n, CHUNK, h).transpose(1, 0, 3, 2)
    beta = beta.reshape(bsz, n, CHUNK, h).transpose(1, 0, 3, 2)
    g = jnp.cumsum(g, axis=-1)

    idx = jnp.arange(CHUNK)
    lower_incl = idx[:, None] >= idx[None, :]
    strict = idx[:, None] > idx[None, :]
    diff = g[..., :, None] - g[..., None, :]
    decay = jnp.where(lower_incl, jnp.exp(jnp.where(lower_incl, diff, 0.0)), 0.0)

    k_beta = k * beta[..., None]
    v_beta = v * beta[..., None]
    L = jnp.where(strict, jnp.einsum("nbhcd,nbhmd->nbhcm", k_beta, k) * decay, 0.0)
    eye = jnp.eye(CHUNK, dtype=jnp.float32)
    rhs = jnp.concatenate([v_beta, k_beta * jnp.exp(g)[..., None]], axis=-1)
    sol = lax.linalg.triangular_solve(eye + L, rhs, left_side=True, lower=True, unit_diagonal=True)
    u, w = sol[..., :dv], sol[..., dv:]
    attn_intra = jnp.where(lower_incl, jnp.einsum("nbhcd,nbhmd->nbhcm", q, k) * decay, 0.0)

    def step(state, inp):
        q_c, k_c, u_c, w_c, g_c, a_c = inp
        v_new = u_c - jnp.einsum("bhck,bhkv->bhcv", w_c, state)
        o = (jnp.einsum("bhck,bhkv->bhcv", q_c * jnp.exp(g_c)[..., None], state)
             + jnp.einsum("bhcm,bhmv->bhcv", a_c, v_new))
        g_last = g_c[..., -1]
        k_dec = k_c * jnp.exp(g_last[..., None] - g_c)[..., None]
        state = state * jnp.exp(g_last)[..., None, None] + jnp.einsum("bhck,bhcv->bhkv", k_dec, v_new)
        return state, o

    s0 = jnp.zeros((bsz, h, dk, dv), jnp.float32)
    _, o = lax.scan(step, s0, (q, k, u, w, g, attn_intra))
    return o.transpose(1, 0, 3, 2, 4).reshape(bsz, t_len, h, dv)


def gated_deltanet(qkv, z, b, a, conv_w, a_log, dt_bias, norm_w):
    bsz, t_len, _ = qkv.shape
    qkv = jax.nn.silu(causal_dwconv(qkv, conv_w)).astype(jnp.float32)
    q, k, v = jnp.split(qkv, 3, axis=-1)
    shp = (bsz, t_len, GDN_HEADS, GDN_HEAD_DIM)
    q = l2norm(q.reshape(shp))
    k = l2norm(k.reshape(shp))
    v = v.reshape(shp)
    beta = jax.nn.sigmoid(b.astype(jnp.float32))
    g = -jnp.exp(a_log.astype(jnp.float32)) * jax.nn.softplus(
        a.astype(jnp.float32) + dt_bias.astype(jnp.float32))
    o = chunk_gated_delta_rule(q, k, v, g, beta)
    o = o * lax.rsqrt(jnp.mean(o * o, axis=-1, keepdims=True) + NORM_EPS) * norm_w.astype(jnp.float32)
    o = o * jax.nn.silu(z.astype(jnp.float32).reshape(shp))
    return o.reshape(bsz, t_len, GDN_WIDTH).astype(z.dtype)


def multiscale_pool(p, pool_w, pool_scale):
    bsz, t_len, _ = p.shape
    pg = p.astype(jnp.float32).reshape(bsz, t_len, POOL_GROUPS, POOL_GROUP_DIM)
    cs = jnp.cumsum(pg, axis=1)
    pos = jnp.arange(t_len)
    means = []
    for gi, win in enumerate(POOL_WINDOWS):
        c = cs[:, :, gi]
        lag = jnp.pad(c[:, : t_len - win], ((0, 0), (win, 0), (0, 0)))
        cnt = jnp.minimum(pos + 1, win).astype(jnp.float32)[None, :, None]
        means.append((c - lag) / cnt)
    pooled = jnp.stack(means, axis=2) - pg
    mixed = jnp.einsum("btgc,gcd->btgd", pooled.astype(p.dtype), pool_w)
    return mixed.reshape(bsz, t_len, POOL_WIDTH) * pool_scale


def memory_cross_attention(h, mem, wq, wk, wv, wo):
    bsz, t_len, _ = h.shape
    q = (h @ wq).reshape(bsz, t_len, XATTN_HEADS, XATTN_HEAD_DIM)
    k = (mem @ wk).reshape(bsz, mem.shape[1], XATTN_HEADS, XATTN_HEAD_DIM)
    v = (mem @ wv).reshape(bsz, mem.shape[1], XATTN_HEADS, XATTN_HEAD_DIM)
    s = jnp.einsum("bqhd,bmhd->bhqm", q, k).astype(jnp.float32) * (XATTN_HEAD_DIM ** -0.5)
    p = jax.nn.softmax(s, axis=-1).astype(v.dtype)
    o = jnp.einsum("bhqm,bmhd->bqhd", p, v).reshape(bsz, t_len, D_MODEL)
    return o @ wo


def setup_inputs(seed: int = 0) -> dict:
    key = jax.random.key(seed)
    ks = jax.random.split(key, 24)
    f32 = jnp.float32
    nrm = lambda k, shape, scale: jax.random.normal(k, shape, f32) * scale
    x = nrm(ks[0], (BATCH, SEQ, D_MODEL), 1.0)
    mem = nrm(ks[1], (BATCH, MEM_LEN, D_MODEL), 1.0)
    w_in = nrm(ks[2], (DEPTH, D_MODEL, IN_COLS), D_MODEL ** -0.5)
    conv_w = nrm(ks[3], (DEPTH, CONV_K, 3 * GDN_WIDTH), CONV_K ** -0.5)
    a_log = jnp.log(jax.random.uniform(ks[4], (DEPTH, GDN_HEADS), f32, 1.0, 16.0))
    dt = jnp.exp(jax.random.uniform(ks[5], (DEPTH, GDN_HEADS), f32, math.log(1e-3), math.log(1e-1)))
    dt_bias = dt + jnp.log(-jnp.expm1(-dt))
    gdn_norm_w = 1.0 + nrm(ks[6], (DEPTH, GDN_HEAD_DIM), 0.02)
    pool_w = nrm(ks[7], (DEPTH, POOL_GROUPS, POOL_GROUP_DIM, POOL_GROUP_DIM), POOL_GROUP_DIM ** -0.5)
    pool_scale = 1.0 + nrm(ks[8], (DEPTH, POOL_WIDTH), 0.1)
    w_out = nrm(ks[9], (DEPTH, MIX_WIDTH, D_MODEL), MIX_WIDTH ** -0.5 * DEEPNORM_BETA)
    ln1_g = 1.0 + nrm(ks[10], (DEPTH, D_MODEL), 0.02)
    ln1_b = nrm(ks[11], (DEPTH, D_MODEL), 0.02)
    xq_w = nrm(ks[12], (DEPTH, D_MODEL, D_MODEL), D_MODEL ** -0.5)
    xk_w = nrm(ks[13], (DEPTH, D_MODEL, D_MODEL), D_MODEL ** -0.5)
    xv_w = nrm(ks[14], (DEPTH, D_MODEL, D_MODEL), D_MODEL ** -0.5)
    xo_w = nrm(ks[15], (DEPTH, D_MODEL, D_MODEL), D_MODEL ** -0.5 * DEEPNORM_BETA)
    ln2_g = 1.0 + nrm(ks[16], (DEPTH, D_MODEL), 0.02)
    ln2_b = nrm(ks[17], (DEPTH, D_MODEL), 0.02)
    w_up = nrm(ks[18], (DEPTH, D_MODEL, D_FF), D_MODEL ** -0.5)
    w_down = nrm(ks[19], (DEPTH, D_FF, D_MODEL), D_FF ** -0.5 * DEEPNORM_BETA)
    ln3_g = 1.0 + nrm(ks[20], (DEPTH, D_MODEL), 0.02)
    ln3_b = nrm(ks[21], (DEPTH, D_MODEL), 0.02)
    return {"x": x, "mem": mem, "w_in": w_in, "conv_w": conv_w, "a_log": a_log, "dt_bias": dt_bias,
            "gdn_norm_w": gdn_norm_w, "pool_w": pool_w, "pool_scale": pool_scale, "w_out": w_out,
            "ln1_g": ln1_g, "ln1_b": ln1_b, "xq_w": xq_w, "xk_w": xk_w, "xv_w": xv_w, "xo_w": xo_w,
            "ln2_g": ln2_g, "ln2_b": ln2_b, "w_up": w_up, "w_down": w_down, "ln3_g": ln3_g, "ln3_b": ln3_b}


def reference(x, mem, w_in, conv_w, a_log, dt_bias, gdn_norm_w, pool_w, pool_scale, w_out,
              ln1_g, ln1_b, xq_w, xk_w, xv_w, xo_w, ln2_g, ln2_b, w_up, w_down, ln3_g, ln3_b):
    W, H = GDN_WIDTH, GDN_HEADS
    h = x
    for l in range(DEPTH):
        proj = h @ w_in[l]
        qkv = proj[..., : 3 * W]
        z = proj[..., 3 * W: 4 * W]
        b = proj[..., 4 * W: 4 * W + H]
        a = proj[..., 4 * W + H: 4 * W + 2 * H]
        p = proj[..., 4 * W + 2 * H:]
        o_gdn = gated_deltanet(qkv, z, b, a, conv_w[l], a_log[l], dt_bias[l], gdn_norm_w[l])
        o_pool = multiscale_pool(p, pool_w[l], pool_scale[l])
        mix = jnp.concatenate([o_gdn, o_pool], axis=-1) @ w_out[l]
        h = layer_norm(DEEPNORM_ALPHA * h + mix, ln1_g[l], ln1_b[l])
        xa = memory_cross_attention(h, mem, xq_w[l], xk_w[l], xv_w[l], xo_w[l])
        h = layer_norm(DEEPNORM_ALPHA * h + xa, ln2_g[l], ln2_b[l])
        ff = jnp.square(jax.nn.relu(h @ w_up[l])) @ w_down[l]
        h = layer_norm(DEEPNORM_ALPHA * h + ff, ln3_g[l], ln3_b[l])
    return h
```

```python
import functools

import jax
import jax.numpy as jnp
from jax import lax
from jax.experimental import pallas as pl
from jax.experimental.pallas import tpu as pltpu

F32 = jnp.float32
BF16 = jnp.bfloat16

LANE = 128
GDN_CHUNK = 128
POOL_WINDOWS = (2, 4, 8, 16)
XATTN_HEADS = 4
LN_EPS = 1e-5
NORM_EPS = 1e-6
VMEM_LIMIT = 56 * 1024 * 1024

_NT = (((1,), (1,)), ((), ()))


def _dot(a, b):
    return jnp.dot(a, b, preferred_element_type=F32)


def _layer_norm(y, g, b):
    mu = jnp.mean(y, axis=-1, keepdims=True)
    yc = y - mu
    var = jnp.mean(yc * yc, axis=-1, keepdims=True)
    return yc * lax.rsqrt(var + LN_EPS) * g + b


def _silu(x):
    return x * jax.nn.sigmoid(x)


def _matmul_kernel(x_ref, w_ref, o_ref, xb_ref):
    @pl.when(pl.program_id(1) == 0)
    def _():
        xb_ref[...] = x_ref[...].astype(BF16)

    o_ref[...] = _dot(xb_ref[...], w_ref[...]).astype(o_ref.dtype)


def _matmul(x, w, out_dtype, tm, tn):
    m, k = x.shape
    n = w.shape[1]
    return pl.pallas_call(
        _matmul_kernel,
        grid=(m // tm, n // tn),
        in_specs=[pl.BlockSpec((tm, k), lambda i, j: (i, 0)),
                  pl.BlockSpec((k, tn), lambda i, j: (0, j))],
        out_specs=pl.BlockSpec((tm, tn), lambda i, j: (i, j)),
        out_shape=jax.ShapeDtypeStruct((m, n), out_dtype),
        scratch_shapes=[pltpu.VMEM((tm, k), BF16)],
        compiler_params=pltpu.CompilerParams(
            dimension_semantics=("parallel", "arbitrary"), vmem_limit_bytes=VMEM_LIMIT),
        name="matmul_rowcast",
    )(x, w)


def _gates_kernel(ba_ref, alog_ref, dtb_ref, beta_ref, gc_ref):
    t = ba_ref.shape[1]
    c = GDN_CHUNK
    beta_ref[0] = jax.nn.sigmoid(ba_ref[0, :, :LANE])
    g = -jnp.exp(alog_ref[...]) * jax.nn.softplus(ba_ref[0, :, LANE:] + dtb_ref[...])
    row = lax.broadcasted_iota(jnp.int32, (c, c), 0)
    col = lax.broadcasted_iota(jnp.int32, (c, c), 1)
    tri = (row >= col).astype(F32)
    for i in range(t // c):
        gc_ref[0, i * c:(i + 1) * c, :] = jnp.dot(
            tri, g[i * c:(i + 1) * c], precision=lax.Precision.HIGHEST, preferred_element_type=F32)


def _gates(proj, alog_pad, dtb_pad, ba_block):
    b, t, _ = proj.shape
    return pl.pallas_call(
        _gates_kernel,
        grid=(b,),
        in_specs=[pl.BlockSpec((1, t, 2 * LANE), lambda i: (i, 0, ba_block)),
                  pl.BlockSpec((1, LANE), lambda i: (0, 0)),
                  pl.BlockSpec((1, LANE), lambda i: (0, 0))],
        out_specs=[pl.BlockSpec((1, t, LANE), lambda i: (i, 0, 0)),
                   pl.BlockSpec((1, t, LANE), lambda i: (i, 0, 0))],
        out_shape=[jax.ShapeDtypeStruct((b, t, LANE), F32)] * 2,
        compiler_params=pltpu.CompilerParams(dimension_semantics=("parallel",)),
        name="gdn_gates",
    )(proj, alog_pad, dtb_pad)


def _gdn_kernel(q_ref, k_ref, v_ref, z_ref, beta_ref, gc_ref, cwq_ref, cwk_ref, cwv_ref, nw_ref,
                o_ref, pad_ref, u_s, wq_s, ak_s, dl_s, *, hb, cg, scale):
    t = q_ref.shape[1]
    c = GDN_CHUNK
    nc = t // c
    conv_k = cwq_ref.shape[0]
    halo = 8
    jh = pl.program_id(1)

    for slot, src in enumerate((q_ref, k_ref, v_ref)):
        for s in range(hb):
            pad_ref[slot * hb + s, 0:halo, :] = jnp.zeros((halo, LANE), F32)
            pad_ref[slot * hb + s, halo:halo + t, :] = src[0, :, s * LANE:(s + 1) * LANE]

    row = lax.broadcasted_iota(jnp.int32, (c, c), 0)
    col = lax.broadcasted_iota(jnp.int32, (c, c), 1)
    incl = row >= col
    strict = row > col
    eye = (row == col).astype(F32)
    lane = lax.broadcasted_iota(jnp.int32, (c, LANE), 1)

    def prep(s, ci):
        r0 = pl.multiple_of(ci * c, c)
        sl = slice(s * LANE, (s + 1) * LANE)

        def conv_act(slot, cw_ref):
            slab = slot * hb + s
            acc = cw_ref[0:1, sl] * pad_ref[slab, pl.ds(r0 + halo - conv_k + 1, c), :]
            for j in range(1, conv_k):
                acc = acc + cw_ref[j:j + 1, sl] * pad_ref[slab, pl.ds(r0 + halo - conv_k + 1 + j, c), :]
            return _silu(acc)

        q = conv_act(0, cwq_ref)
        k = conv_act(1, cwk_ref)
        v = conv_act(2, cwv_ref)
        q = q * lax.rsqrt(jnp.sum(q * q, axis=-1, keepdims=True) + NORM_EPS) * scale
        k = k * lax.rsqrt(jnp.sum(k * k, axis=-1, keepdims=True) + NORM_EPS)

        hm = lane == (jh * hb + s)
        bcol = jnp.sum(jnp.where(hm, beta_ref[0, pl.ds(r0, c), :], 0.0), axis=1, keepdims=True)
        gcol = jnp.sum(jnp.where(hm, gc_ref[0, pl.ds(r0, c), :], 0.0), axis=1, keepdims=True)
        gcb = jnp.broadcast_to(gcol, (c, LANE))
        gct = gcb.T
        dec = jnp.where(incl, jnp.exp(jnp.where(incl, gcb - gct, 0.0)), 0.0)
        ecol = jnp.exp(gcol)

        kb = k * bcol
        k16 = k.astype(BF16)
        kq = lax.dot_general(jnp.concatenate([kb, q], axis=0).astype(BF16), k16, _NT,
                             preferred_element_type=F32)
        a_low = jnp.where(strict, kq[:c] * dec, 0.0)
        attn = jnp.where(incl, kq[c:] * dec, 0.0)

        x = -a_low
        p = eye + x
        n_sq = (c - 1).bit_length() - 1
        for _ in range(n_sq):
            x16 = x.astype(BF16)
            x = _dot(x16, x16)
            p = p + _dot(p.astype(BF16), x.astype(BF16))

        rhs = jnp.concatenate([v * bcol, kb * ecol], axis=1).astype(BF16)
        uw = _dot(p.astype(BF16), rhs)
        u_s[s, ci] = uw[:, :LANE]
        wq_s[s, ci] = jnp.concatenate([uw[:, LANE:], q * ecol], axis=0).astype(BF16)
        glast = gcb[c - 1:c, :]
        kdec = k * jnp.exp(glast - gcb)
        ak_s[s, ci] = jnp.concatenate([attn, kdec.T], axis=0).astype(BF16)
        dl_s[s, ci] = jnp.broadcast_to(jnp.exp(glast), (8, LANE))

    def prep_body(gi, carry):
        for s in range(hb):
            for cc in range(cg):
                prep(s, gi * cg + cc)
        return carry

    lax.fori_loop(0, nc // cg, prep_body, 0)

    def seq_body(ci, states):
        r0 = pl.multiple_of(ci * c, c)
        new = []
        for s in range(hb):
            sl = slice(s * LANE, (s + 1) * LANE)
            st = states[s]
            r = _dot(wq_s[s, ci], st.astype(BF16))
            vn = (u_s[s, ci] - r[:c]).astype(BF16)
            r2 = _dot(ak_s[s, ci], vn)
            o = r[c:] + r2[:c]
            new.append(st * dl_s[s, ci][0:1, :] + r2[c:])
            z = z_ref[0, pl.ds(r0, c), sl]
            o = o * lax.rsqrt(jnp.mean(o * o, axis=-1, keepdims=True) + NORM_EPS) * nw_ref[...]
            o_ref[0, pl.ds(r0, c), sl] = (o * _silu(z)).astype(o_ref.dtype)
        return tuple(new)

    lax.fori_loop(0, nc, seq_body, tuple(jnp.zeros((LANE, LANE), F32) for _ in range(hb)))


def _gdn(proj, beta, gc, conv_w, norm_w, heads, hb, cg):
    b, t, _ = proj.shape
    nj = heads // hb
    nc = t // GDN_CHUNK
    wblk = hb * LANE
    kern = functools.partial(_gdn_kernel, hb=hb, cg=cg, scale=float(LANE) ** -0.5)
    return pl.pallas_call(
        kern,
        grid=(b, nj),
        in_specs=[pl.BlockSpec((1, t, wblk), lambda i, j: (i, 0, j)),
                  pl.BlockSpec((1, t, wblk), lambda i, j: (i, 0, nj + j)),
                  pl.BlockSpec((1, t, wblk), lambda i, j: (i, 0, 2 * nj + j)),
                  pl.BlockSpec((1, t, wblk), lambda i, j: (i, 0, 3 * nj + j)),
                  pl.BlockSpec((1, t, LANE), lambda i, j: (i, 0, 0)),
                  pl.BlockSpec((1, t, LANE), lambda i, j: (i, 0, 0)),
                  pl.BlockSpec((conv_w.shape[0], wblk), lambda i, j: (0, j)),
                  pl.BlockSpec((conv_w.shape[0], wblk), lambda i, j: (0, nj + j)),
                  pl.BlockSpec((conv_w.shape[0], wblk), lambda i, j: (0, 2 * nj + j)),
                  pl.BlockSpec((1, LANE), lambda i, j: (0, 0))],
        out_specs=pl.BlockSpec((1, t, wblk), lambda i, j: (i, 0, j)),
        out_shape=jax.ShapeDtypeStruct((b, t, heads * LANE), BF16),
        scratch_shapes=[pltpu.VMEM((3 * hb, t + 8, LANE), F32),
                        pltpu.VMEM((hb, nc, GDN_CHUNK, LANE), F32),
                        pltpu.VMEM((hb, nc, 2 * GDN_CHUNK, LANE), BF16),
                        pltpu.VMEM((hb, nc, 2 * GDN_CHUNK, LANE), BF16),
                        pltpu.VMEM((hb, nc, 8, LANE), F32)],
        compiler_params=pltpu.CompilerParams(
            dimension_semantics=("parallel", "arbitrary"), vmem_limit_bytes=VMEM_LIMIT),
        name="gdn_delta_rule",
    )(proj, proj, proj, proj, beta, gc, conv_w, conv_w, conv_w, norm_w)


def _pool_kernel(p_ref, w_ref, sc_ref, o_ref, pad_ref, *, rows):
    t = p_ref.shape[1]
    cgd = w_ref.shape[1]
    halo = 16
    nslab = cgd // LANE
    for g, win in enumerate(POOL_WINDOWS):
        sl = slice(g * cgd, (g + 1) * cgd)
        for k in range(nslab):
            pad_ref[k, 0:halo, :] = jnp.zeros((halo, LANE), F32)
            pad_ref[k, halo:halo + t, :] = p_ref[0, :, g * cgd + k * LANE:g * cgd + (k + 1) * LANE]

        def body(i, carry, win=win, sl=sl, g=g):
            r0 = pl.multiple_of(i * rows, rows)

            def shifted(d):
                return jnp.concatenate(
                    [pad_ref[k, pl.ds(r0 + halo - d, rows), :] for k in range(nslab)], axis=1)

            x = shifted(0)
            ssum = x
            for d in range(1, win):
                ssum = ssum + shifted(d)
            pos = r0 + lax.broadcasted_iota(jnp.int32, (rows, cgd), 0)
            cnt = jnp.minimum(pos + 1, win).astype(F32)
            pooled = ssum / cnt - x
            mixed = _dot(pooled.astype(BF16), w_ref[g]) * sc_ref[:, sl]
            o_ref[0, pl.ds(r0, rows), sl] = mixed.astype(o_ref.dtype)
            return carry

        lax.fori_loop(0, t // rows, body, 0)


def _pool(proj, pool_w16, pool_scale, p_block):
    b, t, _ = proj.shape
    g, cgd, _ = pool_w16.shape
    width = g * cgd
    return pl.pallas_call(
        functools.partial(_pool_kernel, rows=256),
        grid=(b,),
        in_specs=[pl.BlockSpec((1, t, width), lambda i: (i, 0, p_block)),
                  pl.BlockSpec((g, cgd, cgd), lambda i: (0, 0, 0)),
                  pl.BlockSpec((1, width), lambda i: (0, 0))],
        out_specs=pl.BlockSpec((1, t, width), lambda i: (i, 0, 0)),
        out_shape=jax.ShapeDtypeStruct((b, t, width), BF16),
        scratch_shapes=[pltpu.VMEM((cgd // LANE, t + 16, LANE), F32)],
        compiler_params=pltpu.CompilerParams(
            dimension_semantics=("parallel",), vmem_limit_bytes=VMEM_LIMIT),
        name="multiscale_pool",
    )(proj, pool_w16, pool_scale)


def _outproj_kernel(og_ref, op_ref, x_ref, wa_ref, wb_ref, g_ref, b_ref, o_ref, *, alpha):
    mix = _dot(og_ref[...], wa_ref[...]) + _dot(op_ref[...], wb_ref[...])
    o_ref[...] = _layer_norm(alpha * x_ref[...] + mix, g_ref[...], b_ref[...])


def _outproj(og, op, x, wa, wb, g, b, alpha, tm):
    m, d = x.shape
    ka, kb = og.shape[1], op.shape[1]
    const = lambda i: (0, 0)
    return pl.pallas_call(
        functools.partial(_outproj_kernel, alpha=alpha),
        grid=(m // tm,),
        in_specs=[pl.BlockSpec((tm, ka), lambda i: (i, 0)),
                  pl.BlockSpec((tm, kb), lambda i: (i, 0)),
                  pl.BlockSpec((tm, d), lambda i: (i, 0)),
                  pl.BlockSpec((ka, d), const),
                  pl.BlockSpec((kb, d), const),
                  pl.BlockSpec((1, d), const),
                  pl.BlockSpec((1, d), const)],
        out_specs=pl.BlockSpec((tm, d), lambda i: (i, 0)),
        out_shape=jax.ShapeDtypeStruct((m, d), F32),
        compiler_params=pltpu.CompilerParams(
            dimension_semantics=("parallel",), vmem_limit_bytes=VMEM_LIMIT),
        name="outproj_ln",
    )(og, op, x, wa, wb, g, b)


def _xattn_kernel(h_ref, wq_ref, kv_ref, wo_ref, g_ref, b_ref, o_ref, *, alpha, heads):
    h = h_ref[0]
    d = h.shape[-1]
    hd = d // heads
    q = _dot(h.astype(BF16), wq_ref[...])
    outs = []
    for i in range(heads):
        qh = q[:, i * hd:(i + 1) * hd].astype(BF16)
        kh = kv_ref[0, :, i * hd:(i + 1) * hd]
        vh = kv_ref[0, :, d + i * hd:d + (i + 1) * hd]
        s = lax.dot_general(qh, kh, _NT, preferred_element_type=F32) * (float(hd) ** -0.5)
        e = jnp.exp(s - jnp.max(s, axis=-1, keepdims=True))
        p = e / jnp.sum(e, axis=-1, keepdims=True)
        outs.append(_dot(p.astype(BF16), vh).astype(BF16))
    xa = _dot(jnp.concatenate(outs, axis=1), wo_ref[...])
    o_ref[0] = _layer_norm(alpha * h + xa, g_ref[...], b_ref[...])


def _xattn(h, wq, kv, wo, g, b, alpha, tm):
    bsz, t, d = h.shape
    mlen = kv.shape[1]
    const = lambda i, j: (0, 0)
    return pl.pallas_call(
        functools.partial(_xattn_kernel, alpha=alpha, heads=XATTN_HEADS),
        grid=(bsz, t // tm),
        in_specs=[pl.BlockSpec((1, tm, d), lambda i, j: (i, j, 0)),
                  pl.BlockSpec((d, d), const),
                  pl.BlockSpec((1, mlen, 2 * d), lambda i, j: (i, 0, 0)),
                  pl.BlockSpec((d, d), const),
                  pl.BlockSpec((1, d), const),
                  pl.BlockSpec((1, d), const)],
        out_specs=pl.BlockSpec((1, tm, d), lambda i, j: (i, j, 0)),
        out_shape=jax.ShapeDtypeStruct((bsz, t, d), F32),
        compiler_params=pltpu.CompilerParams(
            dimension_semantics=("parallel", "parallel"), vmem_limit_bytes=VMEM_LIMIT),
        name="xattn_ln",
    )(h, wq, kv, wo, g, b)


def _mlp_kernel(h_ref, wu_ref, wd_ref, g_ref, b_ref, o_ref, hb_ref, acc_ref, *, alpha):
    j = pl.program_id(1)

    @pl.when(j == 0)
    def _():
        hb_ref[...] = h_ref[...].astype(BF16)
        acc_ref[...] = jnp.zeros_like(acc_ref)

    u = jnp.maximum(_dot(hb_ref[...], wu_ref[...]), 0.0)
    acc_ref[...] += _dot((u * u).astype(BF16), wd_ref[...])

    @pl.when(j == pl.num_programs(1) - 1)
    def _():
        o_ref[...] = _layer_norm(alpha * h_ref[...] + acc_ref[...], g_ref[...], b_ref[...])


def _mlp(h, wu, wd, g, b, alpha, tm, tf):
    m, d = h.shape
    f = wu.shape[1]
    const = lambda i, j: (0, 0)
    return pl.pallas_call(
        functools.partial(_mlp_kernel, alpha=alpha),
        grid=(m // tm, f // tf),
        in_specs=[pl.BlockSpec((tm, d), lambda i, j: (i, 0)),
                  pl.BlockSpec((d, tf), lambda i, j: (0, j)),
                  pl.BlockSpec((tf, d), lambda i, j: (j, 0)),
                  pl.BlockSpec((1, d), const),
                  pl.BlockSpec((1, d), const)],
        out_specs=pl.BlockSpec((tm, d), lambda i, j: (i, 0)),
        out_shape=jax.ShapeDtypeStruct((m, d), F32),
        scratch_shapes=[pltpu.VMEM((tm, d), BF16), pltpu.VMEM((tm, d), F32)],
        compiler_params=pltpu.CompilerParams(
            dimension_semantics=("parallel", "arbitrary"), vmem_limit_bytes=VMEM_LIMIT),
        name="mlp_ln",
    )(h, wu, wd, g, b)


def _pad_lanes(v):
    return jnp.pad(v.astype(F32), (0, LANE - v.shape[0]))[None, :]


def kernel(x, mem, w_in, conv_w, a_log, dt_bias, gdn_norm_w, pool_w, pool_scale, w_out, ln1_g, ln1_b,
           xq_w, xk_w, xv_w, xo_w, ln2_g, ln2_b, w_up, w_down, ln3_g, ln3_b):
    bsz, t, d = x.shape
    depth = w_in.shape[0]
    heads = a_log.shape[1]
    gw = conv_w.shape[2] // 3
    pw = pool_w.shape[1] * pool_w.shape[2]
    assert gw == heads * LANE and t % GDN_CHUNK == 0 and heads <= LANE
    alpha = (2.0 * depth) ** 0.25
    row2 = lambda v: v[None, :]

    h = x
    for l in range(depth):
        wl = w_in[l]
        zpad = jnp.zeros((d, LANE - heads), wl.dtype)
        w_cat = jnp.concatenate(
            [wl[:, :4 * gw], wl[:, 4 * gw + 2 * heads:], wl[:, 4 * gw:4 * gw + heads], zpad,
             wl[:, 4 * gw + heads:4 * gw + 2 * heads], zpad], axis=1).astype(BF16)
        ncols = w_cat.shape[1]
        assert (4 * gw) % pw == 0 and (4 * gw + pw) % (2 * LANE) == 0 and ncols % 768 == 0

        proj = _matmul(h.reshape(bsz * t, d), w_cat, F32, tm=1024, tn=768).reshape(bsz, t, ncols)
        beta, gc = _gates(proj, _pad_lanes(a_log[l]), _pad_lanes(dt_bias[l]), (4 * gw + pw) // (2 * LANE))
        o_gdn = _gdn(proj, beta, gc, conv_w[l], row2(gdn_norm_w[l]), heads, hb=2, cg=4)
        o_pool = _pool(proj, pool_w[l].astype(BF16), row2(pool_scale[l]), (4 * gw) // pw)

        wo16 = w_out[l].astype(BF16)
        h1 = _outproj(o_gdn.reshape(bsz * t, gw), o_pool.reshape(bsz * t, pw), h.reshape(bsz * t, d),
                      wo16[:gw], wo16[gw:], row2(ln1_g[l]), row2(ln1_b[l]), alpha, tm=512)

        mlen = mem.shape[1]
        w_kv = jnp.concatenate([xk_w[l], xv_w[l]], axis=1).astype(BF16)
        kv = _matmul(mem.reshape(bsz * mlen, d), w_kv, BF16, tm=1024, tn=1024).reshape(bsz, mlen, 2 * d)
        h2 = _xattn(h1.reshape(bsz, t, d), xq_w[l].astype(BF16), kv, xo_w[l].astype(BF16),
                    row2(ln2_g[l]), row2(ln2_b[l]), alpha, tm=256)

        h3 = _mlp(h2.reshape(bsz * t, d), w_up[l].astype(BF16), w_down[l].astype(BF16),
                  row2(ln3_g[l]), row2(ln3_b[l]), alpha, tm=512, tf=512)
        h = h3.reshape(bsz, t, d)
    return h
```

```python
import functools

import jax
import jax.numpy as jnp
from jax import lax
from jax.experimental import pallas as pl
from jax.experimental.pallas import tpu as pltpu

F32 = jnp.float32
BF16 = jnp.bfloat16

LANE = 128
GDN_CHUNK = 128
POOL_WINDOWS = (2, 4, 8, 16)
XATTN_HEADS = 4
LN_EPS = 1e-5
NORM_EPS = 1e-6
VMEM_LIMIT = 56 * 1024 * 1024

_NT = (((1,), (1,)), ((), ()))


def _dot(a, b):
    return jnp.dot(a, b, preferred_element_type=F32)


def _layer_norm(y, g, b):
    mu = jnp.mean(y, axis=-1, keepdims=True)
    yc = y - mu
    var = jnp.mean(yc * yc, axis=-1, keepdims=True)
    return yc * lax.rsqrt(var + LN_EPS) * g + b


def _silu(x):
    return x * jax.nn.sigmoid(x)


def _matmul_kernel(x_ref, w_ref, o_ref, xb_ref):
    @pl.when(pl.program_id(1) == 0)
    def _():
        xb_ref[...] = x_ref[...].astype(BF16)

    o_ref[...] = _dot(xb_ref[...], w_ref[...]).astype(o_ref.dtype)


def _matmul(x, w, out_dtype, tm, tn):
    m, k = x.shape
    n = w.shape[1]
    return pl.pallas_call(
        _matmul_kernel,
        grid=(m // tm, n // tn),
        in_specs=[pl.BlockSpec((tm, k), lambda i, j: (i, 0)),
                  pl.BlockSpec((k, tn), lambda i, j: (0, j))],
        out_specs=pl.BlockSpec((tm, tn), lambda i, j: (i, j)),
        out_shape=jax.ShapeDtypeStruct((m, n), out_dtype),
        scratch_shapes=[pltpu.VMEM((tm, k), BF16)],
        compiler_params=pltpu.CompilerParams(
            dimension_semantics=("parallel", "arbitrary"), vmem_limit_bytes=VMEM_LIMIT),
        name="matmul_rowcast",
    )(x, w)


def _gates_kernel(ba_ref, alog_ref, dtb_ref, beta_ref, gc_ref):
    t = ba_ref.shape[1]
    c = GDN_CHUNK
    beta_ref[0] = jax.nn.sigmoid(ba_ref[0, :, :LANE])
    g = -jnp.exp(alog_ref[...]) * jax.nn.softplus(ba_ref[0, :, LANE:] + dtb_ref[...])
    row = lax.broadcasted_iota(jnp.int32, (c, c), 0)
    col = lax.broadcasted_iota(jnp.int32, (c, c), 1)
    tri = (row >= col).astype(F32)
    for i in range(t // c):
        gc_ref[0, i * c:(i + 1) * c, :] = jnp.dot(
            tri, g[i * c:(i + 1) * c], precision=lax.Precision.HIGHEST, preferred_element_type=F32)


def _gates(proj, alog_pad, dtb_pad, ba_block):
    b, t, _ = proj.shape
    return pl.pallas_call(
        _gates_kernel,
        grid=(b,),
        in_specs=[pl.BlockSpec((1, t, 2 * LANE), lambda i: (i, 0, ba_block)),
                  pl.BlockSpec((1, LANE), lambda i: (0, 0)),
                  pl.BlockSpec((1, LANE), lambda i: (0, 0))],
        out_specs=[pl.BlockSpec((1, t, LANE), lambda i: (i, 0, 0)),
                   pl.BlockSpec((1, t, LANE), lambda i: (i, 0, 0))],
        out_shape=[jax.ShapeDtypeStruct((b, t, LANE), F32)] * 2,
        compiler_params=pltpu.CompilerParams(dimension_semantics=("parallel",)),
        name="gdn_gates",
    )(proj, alog_pad, dtb_pad)


def _gdn_kernel(q_ref, k_ref, v_ref, z_ref, beta_ref, gc_ref, cwq_ref, cwk_ref, cwv_ref, nw_ref,
                o_ref, pad_ref, u_s, wq_s, ak_s, dl_s, *, hb, cg, scale):
    t = q_ref.shape[1]
    c = GDN_CHUNK
    nc = t // c
    conv_k = cwq_ref.shape[0]
    halo = 8
    jh = pl.program_id(1)

    for slot, src in enumerate((q_ref, k_ref, v_ref)):
        for s in range(hb):
            pad_ref[slot * hb + s, 0:halo, :] = jnp.zeros((halo, LANE), F32)
            pad_ref[slot * hb + s, halo:halo + t, :] = src[0, :, s * LANE:(s + 1) * LANE]

    row = lax.broadcasted_iota(jnp.int32, (c, c), 0)
    col = lax.broadcasted_iota(jnp.int32, (c, c), 1)
    incl = row >= col
    strict = row > col
    eye = (row == col).astype(F32)
    lane = lax.broadcasted_iota(jnp.int32, (c, LANE), 1)

    def prep_group(g, sb):
        st = []
        for s, cc in [(s, cc) for s in range(hb) for cc in range(cg)]:
            ci = g * cg + cc
            r0 = pl.multiple_of(ci * c, c)
            sl = slice(s * LANE, (s + 1) * LANE)

            def conv_act(slot, cw_ref):
                slab = slot * hb + s
                acc = cw_ref[0:1, sl] * pad_ref[slab, pl.ds(r0 + halo - conv_k + 1, c), :]
                for j in range(1, conv_k):
                    acc = acc + cw_ref[j:j + 1, sl] * pad_ref[slab, pl.ds(r0 + halo - conv_k + 1 + j, c), :]
                return _silu(acc)

            q = conv_act(0, cwq_ref)
            k = conv_act(1, cwk_ref)
            v = conv_act(2, cwv_ref)
            q = q * lax.rsqrt(jnp.sum(q * q, axis=-1, keepdims=True) + NORM_EPS) * scale
            k = k * lax.rsqrt(jnp.sum(k * k, axis=-1, keepdims=True) + NORM_EPS)

            hm = lane == (jh * hb + s)
            bcol = jnp.sum(jnp.where(hm, beta_ref[0, pl.ds(r0, c), :], 0.0), axis=1, keepdims=True)
            gcol = jnp.sum(jnp.where(hm, gc_ref[0, pl.ds(r0, c), :], 0.0), axis=1, keepdims=True)
            gcb = jnp.broadcast_to(gcol, (c, LANE))
            gct = gcb.T
            dec = jnp.where(incl, jnp.exp(jnp.where(incl, gcb - gct, 0.0)), 0.0)
            ecol = jnp.exp(gcol)
            kb = k * bcol
            glast = gcb[c - 1:c, :]
            dl_s[sb, s, cc] = jnp.broadcast_to(jnp.exp(glast), (8, LANE))
            st.append(dict(
                s=s, cc=cc, dec=dec,
                kbq=jnp.concatenate([kb, q], axis=0).astype(BF16), k16=k.astype(BF16),
                rhs=jnp.concatenate([v * bcol, kb * ecol], axis=1).astype(BF16),
                qg=(q * ecol).astype(BF16),
                kdt=(k * jnp.exp(glast - gcb)).T.astype(BF16)))

        for d in st:
            kq = lax.dot_general(d["kbq"], d["k16"], _NT, preferred_element_type=F32)
            d["attn"] = jnp.where(incl, kq[c:] * d["dec"], 0.0).astype(BF16)
            d["x"] = -jnp.where(strict, kq[:c] * d["dec"], 0.0)
            d["p"] = eye + d["x"]
        for d in st:
            x16 = d["x"].astype(BF16)
            d["x"] = _dot(x16, x16)
        n_sq = (c - 1).bit_length() - 1
        for n in range(1, n_sq + 1):
            for d in st:
                x16 = d["x"].astype(BF16)
                if n < n_sq:
                    r = _dot(jnp.concatenate([x16, d["p"].astype(BF16)], axis=0), x16)
                    d["x"] = r[:c]
                    d["p"] = d["p"] + r[c:]
                else:
                    d["p"] = d["p"] + _dot(d["p"].astype(BF16), x16)
        for d in st:
            s, cc = d["s"], d["cc"]
            uw = _dot(d["p"].astype(BF16), d["rhs"])
            u_s[sb, s, cc] = uw[:, :LANE]
            wq_s[sb, s, cc] = jnp.concatenate([uw[:, LANE:].astype(BF16), d["qg"]], axis=0)
            ak_s[sb, s, cc] = jnp.concatenate([d["attn"], d["kdt"]], axis=0)

    def seq_group(g, sb, states):
        states = list(states)
        for cc in range(cg):
            r0 = pl.multiple_of((g * cg + cc) * c, c)
            rs = [_dot(wq_s[sb, s, cc], states[s].astype(BF16)) for s in range(hb)]
            vns = [(u_s[sb, s, cc] - rs[s][:c]).astype(BF16) for s in range(hb)]
            r2s = [_dot(ak_s[sb, s, cc], vns[s]) for s in range(hb)]
            for s in range(hb):
                sl = slice(s * LANE, (s + 1) * LANE)
                states[s] = states[s] * dl_s[sb, s, cc][0:1, :] + r2s[s][c:]
                o = rs[s][c:] + r2s[s][:c]
                z = z_ref[0, pl.ds(r0, c), sl]
                o = o * lax.rsqrt(jnp.mean(o * o, axis=-1, keepdims=True) + NORM_EPS) * nw_ref[...]
                o_ref[0, pl.ds(r0, c), sl] = (o * _silu(z)).astype(o_ref.dtype)
        return tuple(states)

    u_s[1] = jnp.zeros(u_s.shape[1:], u_s.dtype)
    wq_s[1] = jnp.zeros(wq_s.shape[1:], wq_s.dtype)
    ak_s[1] = jnp.zeros(ak_s.shape[1:], ak_s.dtype)
    dl_s[1] = jnp.zeros(dl_s.shape[1:], dl_s.dtype)
    ng = nc // cg

    def pipe_body(m, states):
        states = seq_group(jnp.maximum(2 * m - 1, 0), 1, states)
        prep_group(2 * m, 0)
        states = seq_group(2 * m, 0, states)
        prep_group(2 * m + 1, 1)
        return states

    states = lax.fori_loop(0, ng // 2, pipe_body, tuple(jnp.zeros((LANE, LANE), F32) for _ in range(hb)))
    seq_group(ng - 1, 1, states)


def _gdn(proj, beta, gc, conv_w, norm_w, heads, hb, cg):
    b, t, _ = proj.shape
    nj = heads // hb
    assert (t // GDN_CHUNK) % (2 * cg) == 0 and heads % hb == 0
    wblk = hb * LANE
    kern = functools.partial(_gdn_kernel, hb=hb, cg=cg, scale=float(LANE) ** -0.5)
    return pl.pallas_call(
        kern,
        grid=(b, nj),
        in_specs=[pl.BlockSpec((1, t, wblk), lambda i, j: (i, 0, j)),
                  pl.BlockSpec((1, t, wblk), lambda i, j: (i, 0, nj + j)),
                  pl.BlockSpec((1, t, wblk), lambda i, j: (i, 0, 2 * nj + j)),
                  pl.BlockSpec((1, t, wblk), lambda i, j: (i, 0, 3 * nj + j)),
                  pl.BlockSpec((1, t, LANE), lambda i, j: (i, 0, 0)),
                  pl.BlockSpec((1, t, LANE), lambda i, j: (i, 0, 0)),
                  pl.BlockSpec((conv_w.shape[0], wblk), lambda i, j: (0, j)),
                  pl.BlockSpec((conv_w.shape[0], wblk), lambda i, j: (0, nj + j)),
                  pl.BlockSpec((conv_w.shape[0], wblk), lambda i, j: (0, 2 * nj + j)),
                  pl.BlockSpec((1, LANE), lambda i, j: (0, 0))],
        out_specs=pl.BlockSpec((1, t, wblk), lambda i, j: (i, 0, j)),
        out_shape=jax.ShapeDtypeStruct((b, t, heads * LANE), BF16),
        scratch_shapes=[pltpu.VMEM((3 * hb, t + 8, LANE), F32),
                        pltpu.VMEM((2, hb, cg, GDN_CHUNK, LANE), F32),
                        pltpu.VMEM((2, hb, cg, 2 * GDN_CHUNK, LANE), BF16),
                        pltpu.VMEM((2, hb, cg, 2 * GDN_CHUNK, LANE), BF16),
                        pltpu.VMEM((2, hb, cg, 8, LANE), F32)],
        compiler_params=pltpu.CompilerParams(
            dimension_semantics=("parallel", "arbitrary"), vmem_limit_bytes=VMEM_LIMIT),
        name="gdn_delta_rule",
    )(proj, proj, proj, proj, beta, gc, conv_w, conv_w, conv_w, norm_w)


def _pool_kernel(p_ref, w_ref, sc_ref, o_ref, pad_ref, *, rows):
    t = p_ref.shape[1]
    cgd = w_ref.shape[1]
    halo = 16
    nslab = cgd // LANE
    for g, win in enumerate(POOL_WINDOWS):
        sl = slice(g * cgd, (g + 1) * cgd)
        for k in range(nslab):
            pad_ref[k, 0:halo, :] = jnp.zeros((halo, LANE), F32)
            pad_ref[k, halo:halo + t, :] = p_ref[0, :, g * cgd + k * LANE:g * cgd + (k + 1) * LANE]

        def body(i, carry, win=win, sl=sl, g=g):
            r0 = pl.multiple_of(i * rows, rows)

            def shifted(d):
                return jnp.concatenate(
                    [pad_ref[k, pl.ds(r0 + halo - d, rows), :] for k in range(nslab)], axis=1)

            x = shifted(0)
            ssum = x
            for d in range(1, win):
                ssum = ssum + shifted(d)
            pos = r0 + lax.broadcasted_iota(jnp.int32, (rows, cgd), 0)
            cnt = jnp.minimum(pos + 1, win).astype(F32)
            pooled = ssum / cnt - x
            mixed = _dot(pooled.astype(BF16), w_ref[g]) * sc_ref[:, sl]
            o_ref[0, pl.ds(r0, rows), sl] = mixed.astype(o_ref.dtype)
            return carry

        lax.fori_loop(0, t // rows, body, 0)


def _pool(proj, pool_w16, pool_scale, p_block):
    b, t, _ = proj.shape
    g, cgd, _ = pool_w16.shape
    width = g * cgd
    return pl.pallas_call(
        functools.partial(_pool_kernel, rows=256),
        grid=(b,),
        in_specs=[pl.BlockSpec((1, t, width), lambda i: (i, 0, p_block)),
                  pl.BlockSpec((g, cgd, cgd), lambda i: (0, 0, 0)),
                  pl.BlockSpec((1, width), lambda i: (0, 0))],
        out_specs=pl.BlockSpec((1, t, width), lambda i: (i, 0, 0)),
        out_shape=jax.ShapeDtypeStruct((b, t, width), BF16),
        scratch_shapes=[pltpu.VMEM((cgd // LANE, t + 16, LANE), F32)],
        compiler_params=pltpu.CompilerParams(
            dimension_semantics=("parallel",), vmem_limit_bytes=VMEM_LIMIT),
        name="multiscale_pool",
    )(proj, pool_w16, pool_scale)


def _outproj_kernel(og_ref, op_ref, x_ref, wa_ref, wb_ref, g_ref, b_ref, o_ref, *, alpha):
    mix = _dot(og_ref[...], wa_ref[...]) + _dot(op_ref[...], wb_ref[...])
    o_ref[...] = _layer_norm(alpha * x_ref[...] + mix, g_ref[...], b_ref[...])


def _outproj(og, op, x, wa, wb, g, b, alpha, tm):
    m, d = x.shape
    ka, kb = og.shape[1], op.shape[1]
    const = lambda i: (0, 0)
    return pl.pallas_call(
        functools.partial(_outproj_kernel, alpha=alpha),
        grid=(m // tm,),
        in_specs=[pl.BlockSpec((tm, ka), lambda i: (i, 0)),
                  pl.BlockSpec((tm, kb), lambda i: (i, 0)),
                  pl.BlockSpec((tm, d), lambda i: (i, 0)),
                  pl.BlockSpec((ka, d), const),
                  pl.BlockSpec((kb, d), const),
                  pl.BlockSpec((1, d), const),
                  pl.BlockSpec((1, d), const)],
        out_specs=pl.BlockSpec((tm, d), lambda i: (i, 0)),
        out_shape=jax.ShapeDtypeStruct((m, d), F32),
        compiler_params=pltpu.CompilerParams(
            dimension_semantics=("parallel",), vmem_limit_bytes=VMEM_LIMIT),
        name="outproj_ln",
    )(og, op, x, wa, wb, g, b)


def _xattn_kernel(h_ref, wq_ref, kv_ref, wo_ref, g_ref, b_ref, o_ref, *, alpha, heads):
    h = h_ref[0]
    d = h.shape[-1]
    hd = d // heads
    q = _dot(h.astype(BF16), wq_ref[...])
    outs = []
    for i in range(heads):
        qh = q[:, i * hd:(i + 1) * hd].astype(BF16)
        kh = kv_ref[0, :, i * hd:(i + 1) * hd]
        vh = kv_ref[0, :, d + i * hd:d + (i + 1) * hd]
        s = lax.dot_general(qh, kh, _NT, preferred_element_type=F32) * (float(hd) ** -0.5)
        e = jnp.exp(s - jnp.max(s, axis=-1, keepdims=True))
        p = e / jnp.sum(e, axis=-1, keepdims=True)
        outs.append(_dot(p.astype(BF16), vh).astype(BF16))
    xa = _dot(jnp.concatenate(outs, axis=1), wo_ref[...])
    o_ref[0] = _layer_norm(alpha * h + xa, g_ref[...], b_ref[...])


def _xattn(h, wq, kv, wo, g, b, alpha, tm):
    bsz, t, d = h.shape
    mlen = kv.shape[1]
    const = lambda i, j: (0, 0)
    return pl.pallas_call(
        functools.partial(_xattn_kernel, alpha=alpha, heads=XATTN_HEADS),
        grid=(bsz, t // tm),
        in_specs=[pl.BlockSpec((1, tm, d), lambda i, j: (i, j, 0)),
                  pl.BlockSpec((d, d), const),
                  pl.BlockSpec((1, mlen, 2 * d), lambda i, j: (i, 0, 0)),
                  pl.BlockSpec((d, d), const),
                  pl.BlockSpec((1, d), const),
                  pl.BlockSpec((1, d), const)],
        out_specs=pl.BlockSpec((1, tm, d), lambda i, j: (i, j, 0)),
        out_shape=jax.ShapeDtypeStruct((bsz, t, d), F32),
        compiler_params=pltpu.CompilerParams(
            dimension_semantics=("parallel", "parallel"), vmem_limit_bytes=VMEM_LIMIT),
        name="xattn_ln",
    )(h, wq, kv, wo, g, b)


def _mlp_kernel(h_ref, wu_ref, wd_ref, g_ref, b_ref, o_ref, hb_ref, acc_ref, *, alpha):
    j = pl.program_id(1)

    @pl.when(j == 0)
    def _():
        hb_ref[...] = h_ref[...].astype(BF16)
        acc_ref[...] = jnp.zeros_like(acc_ref)

    u = jnp.maximum(_dot(hb_ref[...], wu_ref[...]), 0.0)
    acc_ref[...] += _dot((u * u).astype(BF16), wd_ref[...])

    @pl.when(j == pl.num_programs(1) - 1)
    def _():
        o_ref[...] = _layer_norm(alpha * h_ref[...] + acc_ref[...], g_ref[...], b_ref[...])


def _mlp(h, wu, wd, g, b, alpha, tm, tf):
    m, d = h.shape
    f = wu.shape[1]
    const = lambda i, j: (0, 0)
    return pl.pallas_call(
        functools.partial(_mlp_kernel, alpha=alpha),
        grid=(m // tm, f // tf),
        in_specs=[pl.BlockSpec((tm, d), lambda i, j: (i, 0)),
                  pl.BlockSpec((d, tf), lambda i, j: (0, j)),
                  pl.BlockSpec((tf, d), lambda i, j: (j, 0)),
                  pl.BlockSpec((1, d), const),
                  pl.BlockSpec((1, d), const)],
        out_specs=pl.BlockSpec((tm, d), lambda i, j: (i, 0)),
        out_shape=jax.ShapeDtypeStruct((m, d), F32),
        scratch_shapes=[pltpu.VMEM((tm, d), BF16), pltpu.VMEM((tm, d), F32)],
        compiler_params=pltpu.CompilerParams(
            dimension_semantics=("parallel", "arbitrary"), vmem_limit_bytes=VMEM_LIMIT),
        name="mlp_ln",
    )(h, wu, wd, g, b)


def _pad_lanes(v):
    return jnp.pad(v.astype(F32), (0, LANE - v.shape[0]))[None, :]


def kernel(x, mem, w_in, conv_w, a_log, dt_bias, gdn_norm_w, pool_w, pool_scale, w_out, ln1_g, ln1_b,
           xq_w, xk_w, xv_w, xo_w, ln2_g, ln2_b, w_up, w_down, ln3_g, ln3_b):
    bsz, t, d = x.shape
    depth = w_in.shape[0]
    heads = a_log.shape[1]
    gw = conv_w.shape[2] // 3
    pw = pool_w.shape[1] * pool_w.shape[2]
    assert gw == heads * LANE and t % GDN_CHUNK == 0 and heads <= LANE
    alpha = (2.0 * depth) ** 0.25
    row2 = lambda v: v[None, :]

    h = x
    for l in range(depth):
        wl = w_in[l]
        zpad = jnp.zeros((d, LANE - heads), wl.dtype)
        w_cat = jnp.concatenate(
            [wl[:, :4 * gw], wl[:, 4 * gw + 2 * heads:], wl[:, 4 * gw:4 * gw + heads], zpad,
             wl[:, 4 * gw + heads:4 * gw + 2 * heads], zpad], axis=1).astype(BF16)
        ncols = w_cat.shape[1]
        assert (4 * gw) % pw == 0 and (4 * gw + pw) % (2 * LANE) == 0 and ncols % 768 == 0

        proj = _matmul(h.reshape(bsz * t, d), w_cat, F32, tm=1024, tn=768).reshape(bsz, t, ncols)
        beta, gc = _gates(proj, _pad_lanes(a_log[l]), _pad_lanes(dt_bias[l]), (4 * gw + pw) // (2 * LANE))
        o_gdn = _gdn(proj, beta, gc, conv_w[l], row2(gdn_norm_w[l]), heads, hb=2, cg=4)
        o_pool = _pool(proj, pool_w[l].astype(BF16), row2(pool_scale[l]), (4 * gw) // pw)

        wo16 = w_out[l].astype(BF16)
        h1 = _outproj(o_gdn.reshape(bsz * t, gw), o_pool.reshape(bsz * t, pw), h.reshape(bsz * t, d),
                      wo16[:gw], wo16[gw:], row2(ln1_g[l]), row2(ln1_b[l]), alpha, tm=512)

        mlen = mem.shape[1]
        w_kv = jnp.concatenate([xk_w[l], xv_w[l]], axis=1).astype(BF16)
        kv = _matmul(mem.reshape(bsz * mlen, d), w_kv, BF16, tm=1024, tn=1024).reshape(bsz, mlen, 2 * d)
        h2 = _xattn(h1.reshape(bsz, t, d), xq_w[l].astype(BF16), kv, xo_w[l].astype(BF16),
                    row2(ln2_g[l]), row2(ln2_b[l]), alpha, tm=256)

        h3 = _mlp(h2.reshape(bsz * t, d), w_up[l].astype(BF16), w_down[l].astype(BF16),
                  row2(ln3_g[l]), row2(ln3_b[l]), alpha, tm=512, tf=512)
        h = h3.reshape(bsz, t, d)
    return h
```

```python
import functools

import jax
import jax.numpy as jnp
from jax import lax
from jax.experimental import pallas as pl
from jax.experimental.pallas import tpu as pltpu

F32 = jnp.float32
BF16 = jnp.bfloat16

LANE = 128
GDN_CHUNK = 128
POOL_WINDOWS = (2, 4, 8, 16)
XATTN_HEADS = 4
LN_EPS = 1e-5
NORM_EPS = 1e-6
VMEM_LIMIT = 56 * 1024 * 1024

_NT = (((1,), (1,)), ((), ()))


def _dot(a, b):
    return jnp.dot(a, b, preferred_element_type=F32)


def _layer_norm(y, g, b):
    mu = jnp.mean(y, axis=-1, keepdims=True)
    yc = y - mu
    var = jnp.mean(yc * yc, axis=-1, keepdims=True)
    return yc * lax.rsqrt(var + LN_EPS) * g + b


def _silu(x):
    return x * jax.nn.sigmoid(x)


def _matmul_kernel(x_ref, w_ref, o_ref, xb_ref):
    @pl.when(pl.program_id(1) == 0)
    def _():
        xb_ref[...] = x_ref[...].astype(BF16)

    o_ref[...] = _dot(xb_ref[...], w_ref[...]).astype(o_ref.dtype)


def _matmul(x, w, out_dtype, tm, tn):
    m, k = x.shape
    n = w.shape[1]
    return pl.pallas_call(
        _matmul_kernel,
        grid=(m // tm, n // tn),
        in_specs=[pl.BlockSpec((tm, k), lambda i, j: (i, 0)),
                  pl.BlockSpec((k, tn), lambda i, j: (0, j))],
        out_specs=pl.BlockSpec((tm, tn), lambda i, j: (i, j)),
        out_shape=jax.ShapeDtypeStruct((m, n), out_dtype),
        scratch_shapes=[pltpu.VMEM((tm, k), BF16)],
        compiler_params=pltpu.CompilerParams(
            dimension_semantics=("parallel", "arbitrary"), vmem_limit_bytes=VMEM_LIMIT),
        name="matmul_rowcast",
    )(x, w)


def _inproj_kernel(x_ref, wm_ref, wt_ref, om_ref, ot_ref, xb_ref, *, nmain):
    j = pl.program_id(1)

    @pl.when(j == 0)
    def _():
        xb_ref[...] = x_ref[...].astype(BF16)

    @pl.when(j < nmain)
    def _():
        om_ref[...] = _dot(xb_ref[...], wm_ref[...])

    @pl.when(j == nmain)
    def _():
        ot_ref[...] = _dot(xb_ref[...], wt_ref[...])


def _inproj(x, w_all, n_main, w_tail, tm, tn):
    m, k = x.shape
    nt = w_tail.shape[1]
    nmain = n_main // tn
    main_idx = lambda j: jnp.minimum(j, nmain - 1)
    return pl.pallas_call(
        functools.partial(_inproj_kernel, nmain=nmain),
        grid=(m // tm, nmain + 1),
        in_specs=[pl.BlockSpec((tm, k), lambda i, j: (i, 0)),
                  pl.BlockSpec((k, tn), lambda i, j: (0, main_idx(j))),
                  pl.BlockSpec((k, nt), lambda i, j: (0, 0), pipeline_mode=pl.Buffered(1))],
        out_specs=[pl.BlockSpec((tm, tn), lambda i, j: (i, main_idx(j))),
                   pl.BlockSpec((tm, nt), lambda i, j: (i, 0))],
        out_shape=[jax.ShapeDtypeStruct((m, n_main), F32), jax.ShapeDtypeStruct((m, nt), F32)],
        scratch_shapes=[pltpu.VMEM((tm, k), BF16)],
        compiler_params=pltpu.CompilerParams(
            dimension_semantics=("parallel", "arbitrary"), vmem_limit_bytes=VMEM_LIMIT),
        name="inproj",
    )(x, w_all, w_tail)


def _gates_kernel(ba_ref, alog_ref, dtb_ref, beta_ref, gc_ref):
    t = ba_ref.shape[1]
    c = GDN_CHUNK
    beta_ref[0] = jax.nn.sigmoid(ba_ref[0, :, :LANE])
    g = -jnp.exp(alog_ref[...]) * jax.nn.softplus(ba_ref[0, :, LANE:] + dtb_ref[...])
    row = lax.broadcasted_iota(jnp.int32, (c, c), 0)
    col = lax.broadcasted_iota(jnp.int32, (c, c), 1)
    tri = (row >= col).astype(F32)
    for i in range(t // c):
        gc_ref[0, i * c:(i + 1) * c, :] = jnp.dot(
            tri, g[i * c:(i + 1) * c], precision=lax.Precision.HIGHEST, preferred_element_type=F32)


def _gates(proj, alog_pad, dtb_pad, ba_block):
    b, t, _ = proj.shape
    return pl.pallas_call(
        _gates_kernel,
        grid=(b,),
        in_specs=[pl.BlockSpec((1, t, 2 * LANE), lambda i: (i, 0, ba_block)),
                  pl.BlockSpec((1, LANE), lambda i: (0, 0)),
                  pl.BlockSpec((1, LANE), lambda i: (0, 0))],
        out_specs=[pl.BlockSpec((1, t, LANE), lambda i: (i, 0, 0)),
                   pl.BlockSpec((1, t, LANE), lambda i: (i, 0, 0))],
        out_shape=[jax.ShapeDtypeStruct((b, t, LANE), F32)] * 2,
        compiler_params=pltpu.CompilerParams(dimension_semantics=("parallel",)),
        name="gdn_gates",
    )(proj, alog_pad, dtb_pad)


def _gdn_kernel(q_ref, k_ref, v_ref, z_ref, beta_ref, gc_ref, cwq_ref, cwk_ref, cwv_ref, nw_ref,
                o_ref, pad_ref, u_s, wq_s, ak_s, dl_s, *, hb, cg, scale):
    t = q_ref.shape[1]
    c = GDN_CHUNK
    nc = t // c
    conv_k = cwq_ref.shape[0]
    halo = 8
    jh = pl.program_id(1)

    for slot, src in enumerate((q_ref, k_ref, v_ref)):
        for s in range(hb):
            pad_ref[slot * hb + s, 0:halo, :] = jnp.zeros((halo, LANE), F32)
            pad_ref[slot * hb + s, halo:halo + t, :] = src[0, :, s * LANE:(s + 1) * LANE]

    row = lax.broadcasted_iota(jnp.int32, (c, c), 0)
    col = lax.broadcasted_iota(jnp.int32, (c, c), 1)
    incl = row >= col
    strict = row > col
    eye = (row == col).astype(F32)
    lane = lax.broadcasted_iota(jnp.int32, (c, LANE), 1)

    n_sq = (c - 1).bit_length() - 1
    n_prep = hb * cg * (4 + n_sq)
    n_seq = cg * (2 + hb)

    def prep_group(g, sb):
        st = []
        for s, cc in [(s, cc) for s in range(hb) for cc in range(cg)]:
            ci = g * cg + cc
            r0 = pl.multiple_of(ci * c, c)
            sl = slice(s * LANE, (s + 1) * LANE)

            def conv_act(slot, cw_ref):
                slab = slot * hb + s
                acc = cw_ref[0:1, sl] * pad_ref[slab, pl.ds(r0 + halo - conv_k + 1, c), :]
                for j in range(1, conv_k):
                    acc = acc + cw_ref[j:j + 1, sl] * pad_ref[slab, pl.ds(r0 + halo - conv_k + 1 + j, c), :]
                return _silu(acc)

            q = conv_act(0, cwq_ref)
            k = conv_act(1, cwk_ref)
            v = conv_act(2, cwv_ref)
            q = q * lax.rsqrt(jnp.sum(q * q, axis=-1, keepdims=True) + NORM_EPS) * scale
            k = k * lax.rsqrt(jnp.sum(k * k, axis=-1, keepdims=True) + NORM_EPS)

            hm = lane == (jh * hb + s)
            bcol = jnp.sum(jnp.where(hm, beta_ref[0, pl.ds(r0, c), :], 0.0), axis=1, keepdims=True)
            gcol = jnp.sum(jnp.where(hm, gc_ref[0, pl.ds(r0, c), :], 0.0), axis=1, keepdims=True)
            gcb = jnp.broadcast_to(gcol, (c, LANE))
            gct = gcb.T
            dec = jnp.where(incl, jnp.exp(jnp.where(incl, gcb - gct, 0.0)), 0.0)
            ecol = jnp.exp(gcol)
            kb = k * bcol
            glast = gcb[c - 1:c, :]
            dl_s[sb, s, cc] = jnp.broadcast_to(jnp.exp(glast), (8, LANE))
            st.append(dict(
                s=s, cc=cc, dec=dec,
                kbq=jnp.concatenate([kb, q], axis=0).astype(BF16), k16=k.astype(BF16),
                rhs=jnp.concatenate([v * bcol, kb * ecol], axis=1).astype(BF16),
                qg=(q * ecol).astype(BF16),
                kdt=(k * jnp.exp(glast - gcb)).T.astype(BF16)))
            yield

        for d in st:
            kq = lax.dot_general(d["kbq"], d["k16"], _NT, preferred_element_type=F32)
            d["attn"] = jnp.where(incl, kq[c:] * d["dec"], 0.0).astype(BF16)
            d["x"] = -jnp.where(strict, kq[:c] * d["dec"], 0.0)
            d["p"] = eye + d["x"]
            yield
        for d in st:
            x16 = d["x"].astype(BF16)
            d["x"] = _dot(x16, x16)
            yield
        for n in range(1, n_sq + 1):
            for d in st:
                x16 = d["x"].astype(BF16)
                if n < n_sq:
                    r = _dot(jnp.concatenate([x16, d["p"].astype(BF16)], axis=0), x16)
                    d["x"] = r[:c]
                    d["p"] = d["p"] + r[c:]
                else:
                    d["p"] = d["p"] + _dot(d["p"].astype(BF16), x16)
                yield
        for d in st:
            s, cc = d["s"], d["cc"]
            uw = _dot(d["p"].astype(BF16), d["rhs"])
            u_s[sb, s, cc] = uw[:, :LANE]
            wq_s[sb, s, cc] = jnp.concatenate([uw[:, LANE:].astype(BF16), d["qg"]], axis=0)
            ak_s[sb, s, cc] = jnp.concatenate([d["attn"], d["kdt"]], axis=0)
            yield

    def seq_group(g, sb, states):
        for cc in range(cg):
            r0 = pl.multiple_of((g * cg + cc) * c, c)
            rs = [_dot(wq_s[sb, s, cc], states[s].astype(BF16)) for s in range(hb)]
            yield
            vns = [(u_s[sb, s, cc] - rs[s][:c]).astype(BF16) for s in range(hb)]
            r2s = [_dot(ak_s[sb, s, cc], vns[s]) for s in range(hb)]
            yield
            for s in range(hb):
                sl = slice(s * LANE, (s + 1) * LANE)
                states[s] = states[s] * dl_s[sb, s, cc][0:1, :] + r2s[s][c:]
                o = rs[s][c:] + r2s[s][:c]
                z = z_ref[0, pl.ds(r0, c), sl]
                o = o * lax.rsqrt(jnp.mean(o * o, axis=-1, keepdims=True) + NORM_EPS) * nw_ref[...]
                o_ref[0, pl.ds(r0, c), sl] = (o * _silu(z)).astype(o_ref.dtype)
                yield

    def weave(seq, prep):
        done = 0
        for i in range(n_prep):
            while done * n_prep <= i * n_seq and done < n_seq:
                next(seq)
                done += 1
            next(prep)
        for _ in range(done, n_seq):
            next(seq)

    u_s[1] = jnp.zeros(u_s.shape[1:], u_s.dtype)
    wq_s[1] = jnp.zeros(wq_s.shape[1:], wq_s.dtype)
    ak_s[1] = jnp.zeros(ak_s.shape[1:], ak_s.dtype)
    dl_s[1] = jnp.zeros(dl_s.shape[1:], dl_s.dtype)
    ng = nc // cg

    def pipe_body(m, states):
        states = list(states)
        weave(seq_group(jnp.maximum(2 * m - 1, 0), 1, states), prep_group(2 * m, 0))
        weave(seq_group(2 * m, 0, states), prep_group(2 * m + 1, 1))
        return tuple(states)

    states = lax.fori_loop(0, ng // 2, pipe_body, tuple(jnp.zeros((LANE, LANE), F32) for _ in range(hb)))
    for _ in seq_group(ng - 1, 1, list(states)):
        pass


def _gdn(proj, beta, gc, conv_w, norm_w, heads, hb, cg):
    b, t, _ = proj.shape
    nj = heads // hb
    assert (t // GDN_CHUNK) % (2 * cg) == 0 and heads % hb == 0
    wblk = hb * LANE
    kern = functools.partial(_gdn_kernel, hb=hb, cg=cg, scale=float(LANE) ** -0.5)
    return pl.pallas_call(
        kern,
        grid=(b, nj),
        in_specs=[pl.BlockSpec((1, t, wblk), lambda i, j: (i, 0, j)),
                  pl.BlockSpec((1, t, wblk), lambda i, j: (i, 0, nj + j)),
                  pl.BlockSpec((1, t, wblk), lambda i, j: (i, 0, 2 * nj + j)),
                  pl.BlockSpec((1, t, wblk), lambda i, j: (i, 0, 3 * nj + j)),
                  pl.BlockSpec((1, t, LANE), lambda i, j: (i, 0, 0)),
                  pl.BlockSpec((1, t, LANE), lambda i, j: (i, 0, 0)),
                  pl.BlockSpec((conv_w.shape[0], wblk), lambda i, j: (0, j)),
                  pl.BlockSpec((conv_w.shape[0], wblk), lambda i, j: (0, nj + j)),
                  pl.BlockSpec((conv_w.shape[0], wblk), lambda i, j: (0, 2 * nj + j)),
                  pl.BlockSpec((1, LANE), lambda i, j: (0, 0))],
        out_specs=pl.BlockSpec((1, t, wblk), lambda i, j: (i, 0, j)),
        out_shape=jax.ShapeDtypeStruct((b, t, heads * LANE), BF16),
        scratch_shapes=[pltpu.VMEM((3 * hb, t + 8, LANE), F32),
                        pltpu.VMEM((2, hb, cg, GDN_CHUNK, LANE), F32),
                        pltpu.VMEM((2, hb, cg, 2 * GDN_CHUNK, LANE), BF16),
                        pltpu.VMEM((2, hb, cg, 2 * GDN_CHUNK, LANE), BF16),
                        pltpu.VMEM((2, hb, cg, 8, LANE), F32)],
        compiler_params=pltpu.CompilerParams(
            dimension_semantics=("parallel", "arbitrary"), vmem_limit_bytes=VMEM_LIMIT),
        name="gdn_delta_rule",
    )(proj, proj, proj, proj, beta, gc, conv_w, conv_w, conv_w, norm_w)


def _pool_kernel(p_ref, w_ref, sc_ref, o_ref, pad_ref, *, rows):
    t = p_ref.shape[1]
    cgd = w_ref.shape[1]
    halo = 16
    nslab = cgd // LANE
    for g, win in enumerate(POOL_WINDOWS):
        sl = slice(g * cgd, (g + 1) * cgd)
        for k in range(nslab):
            pad_ref[k, 0:halo, :] = jnp.zeros((halo, LANE), F32)
            pad_ref[k, halo:halo + t, :] = p_ref[0, :, g * cgd + k * LANE:g * cgd + (k + 1) * LANE]

        def body(i, carry, win=win, sl=sl, g=g):
            r0 = pl.multiple_of(i * rows, rows)

            def shifted(d):
                return jnp.concatenate(
                    [pad_ref[k, pl.ds(r0 + halo - d, rows), :] for k in range(nslab)], axis=1)

            x = shifted(0)
            ssum = x
            for d in range(1, win):
                ssum = ssum + shifted(d)
            pos = r0 + lax.broadcasted_iota(jnp.int32, (rows, cgd), 0)
            cnt = jnp.minimum(pos + 1, win).astype(F32)
            pooled = ssum / cnt - x
            mixed = _dot(pooled.astype(BF16), w_ref[g]) * sc_ref[:, sl]
            o_ref[0, pl.ds(r0, rows), sl] = mixed.astype(o_ref.dtype)
            return carry

        lax.fori_loop(0, t // rows, body, 0)


def _pool(proj, pool_w16, pool_scale, p_block):
    b, t, _ = proj.shape
    g, cgd, _ = pool_w16.shape
    width = g * cgd
    return pl.pallas_call(
        functools.partial(_pool_kernel, rows=256),
        grid=(b,),
        in_specs=[pl.BlockSpec((1, t, width), lambda i: (i, 0, p_block)),
                  pl.BlockSpec((g, cgd, cgd), lambda i: (0, 0, 0)),
                  pl.BlockSpec((1, width), lambda i: (0, 0))],
        out_specs=pl.BlockSpec((1, t, width), lambda i: (i, 0, 0)),
        out_shape=jax.ShapeDtypeStruct((b, t, width), BF16),
        scratch_shapes=[pltpu.VMEM((cgd // LANE, t + 16, LANE), F32)],
        compiler_params=pltpu.CompilerParams(
            dimension_semantics=("parallel",), vmem_limit_bytes=VMEM_LIMIT),
        name="multiscale_pool",
    )(proj, pool_w16, pool_scale)


def _outproj_kernel(og_ref, op_ref, x_ref, wa_ref, wb_ref, g_ref, b_ref, o_ref, *, alpha, rb):
    for r in range(o_ref.shape[0] // rb):
        rows = slice(r * rb, (r + 1) * rb)
        mix = _dot(og_ref[rows, :], wa_ref[...]) + _dot(op_ref[rows, :], wb_ref[...])
        o_ref[rows, :] = _layer_norm(alpha * x_ref[rows, :] + mix, g_ref[...], b_ref[...])


def _outproj(og, op, x, w, g, b, alpha, tm, rb):
    m, d = x.shape
    ka, kb = og.shape[1], op.shape[1]
    assert ka % kb == 0 and w.shape[0] == ka + kb
    const = lambda i: (0, 0)
    once = pl.Buffered(1)
    return pl.pallas_call(
        functools.partial(_outproj_kernel, alpha=alpha, rb=rb),
        grid=(m // tm,),
        in_specs=[pl.BlockSpec((tm, ka), lambda i: (i, 0)),
                  pl.BlockSpec((tm, kb), lambda i: (i, 0)),
                  pl.BlockSpec((tm, d), lambda i: (i, 0)),
                  pl.BlockSpec((ka, d), const, pipeline_mode=once),
                  pl.BlockSpec((kb, d), lambda i: (ka // kb, 0), pipeline_mode=once),
                  pl.BlockSpec((1, d), const),
                  pl.BlockSpec((1, d), const)],
        out_specs=pl.BlockSpec((tm, d), lambda i: (i, 0)),
        out_shape=jax.ShapeDtypeStruct((m, d), F32),
        compiler_params=pltpu.CompilerParams(
            dimension_semantics=("parallel",), vmem_limit_bytes=VMEM_LIMIT),
        name="outproj_ln",
    )(og, op, x, w, w, g, b)


def _xattn_kernel(h_ref, wq_ref, kv_ref, wo_ref, g_ref, b_ref, o_ref, *, alpha, heads, rb):
    d = h_ref.shape[-1]
    hd = d // heads
    for r in range(h_ref.shape[1] // rb):
        rows = slice(r * rb, (r + 1) * rb)
        h = h_ref[0, rows, :]
        q = _dot(h.astype(BF16), wq_ref[...])
        outs = []
        for i in range(heads):
            qh = q[:, i * hd:(i + 1) * hd].astype(BF16)
            kh = kv_ref[0, :, i * hd:(i + 1) * hd]
            vh = kv_ref[0, :, d + i * hd:d + (i + 1) * hd]
            s = lax.dot_general(qh, kh, _NT, preferred_element_type=F32) * (float(hd) ** -0.5)
            e = jnp.exp(s - jnp.max(s, axis=-1, keepdims=True))
            p = e / jnp.sum(e, axis=-1, keepdims=True)
            outs.append(_dot(p.astype(BF16), vh).astype(BF16))
        xa = _dot(jnp.concatenate(outs, axis=1), wo_ref[...])
        o_ref[0, rows, :] = _layer_norm(alpha * h + xa, g_ref[...], b_ref[...])


def _xattn(h, wq, kv, wo, g, b, alpha, tm, rb):
    bsz, t, d = h.shape
    mlen = kv.shape[1]
    const = lambda i, j: (0, 0)
    once = pl.Buffered(1)
    return pl.pallas_call(
        functools.partial(_xattn_kernel, alpha=alpha, heads=XATTN_HEADS, rb=rb),
        grid=(bsz, t // tm),
        in_specs=[pl.BlockSpec((1, tm, d), lambda i, j: (i, j, 0)),
                  pl.BlockSpec((d, d), const, pipeline_mode=once),
                  pl.BlockSpec((1, mlen, 2 * d), lambda i, j: (i, 0, 0)),
                  pl.BlockSpec((d, d), const, pipeline_mode=once),
                  pl.BlockSpec((1, d), const),
                  pl.BlockSpec((1, d), const)],
        out_specs=pl.BlockSpec((1, tm, d), lambda i, j: (i, j, 0)),
        out_shape=jax.ShapeDtypeStruct((bsz, t, d), F32),
        compiler_params=pltpu.CompilerParams(
            dimension_semantics=("parallel", "parallel"), vmem_limit_bytes=VMEM_LIMIT),
        name="xattn_ln",
    )(h, wq, kv, wo, g, b)


def _mlp_kernel(h_ref, wu_ref, wd_ref, g_ref, b_ref, o_ref, hb_ref, *, alpha, rb):
    j = pl.program_id(1)

    @pl.when(j == 0)
    def _():
        hb_ref[...] = h_ref[...].astype(BF16)
        o_ref[...] = jnp.zeros_like(o_ref)

    u = jnp.maximum(_dot(hb_ref[...], wu_ref[...]), 0.0)
    o_ref[...] += _dot((u * u).astype(BF16), wd_ref[...])

    @pl.when(j == pl.num_programs(1) - 1)
    def _():
        for r in range(o_ref.shape[0] // rb):
            rows = slice(r * rb, (r + 1) * rb)
            o_ref[rows, :] = _layer_norm(alpha * h_ref[rows, :] + o_ref[rows, :], g_ref[...], b_ref[...])


def _mlp(h, wu, wd, g, b, alpha, tm, tf, rb):
    m, d = h.shape
    f = wu.shape[1]
    const = lambda i, j: (0, 0)
    return pl.pallas_call(
        functools.partial(_mlp_kernel, alpha=alpha, rb=rb),
        grid=(m // tm, f // tf),
        in_specs=[pl.BlockSpec((tm, d), lambda i, j: (i, 0)),
                  pl.BlockSpec((d, tf), lambda i, j: (0, j)),
                  pl.BlockSpec((tf, d), lambda i, j: (j, 0)),
                  pl.BlockSpec((1, d), const),
                  pl.BlockSpec((1, d), const)],
        out_specs=pl.BlockSpec((tm, d), lambda i, j: (i, 0)),
        out_shape=jax.ShapeDtypeStruct((m, d), F32),
        scratch_shapes=[pltpu.VMEM((tm, d), BF16)],
        compiler_params=pltpu.CompilerParams(
            dimension_semantics=("parallel", "arbitrary"), vmem_limit_bytes=VMEM_LIMIT),
        name="mlp_ln",
    )(h, wu, wd, g, b)


def _pad_lanes(v):
    return jnp.pad(v.astype(F32), (0, LANE - v.shape[0]))[None, :]


def kernel(x, mem, w_in, conv_w, a_log, dt_bias, gdn_norm_w, pool_w, pool_scale, w_out, ln1_g, ln1_b,
           xq_w, xk_w, xv_w, xo_w, ln2_g, ln2_b, w_up, w_down, ln3_g, ln3_b):
    bsz, t, d = x.shape
    depth = w_in.shape[0]
    heads = a_log.shape[1]
    gw = conv_w.shape[2] // 3
    pw = pool_w.shape[1] * pool_w.shape[2]
    assert gw == heads * LANE and t % GDN_CHUNK == 0 and heads <= LANE
    alpha = (2.0 * depth) ** 0.25
    row2 = lambda v: v[None, :]

    h = x
    for l in range(depth):
        w16 = w_in[l].astype(BF16)
        zpad = jnp.zeros((d, LANE - heads), BF16)
        w_tail = jnp.concatenate(
            [w16[:, 4 * gw + 2 * heads:], w16[:, 4 * gw:4 * gw + heads], zpad,
             w16[:, 4 * gw + heads:4 * gw + 2 * heads], zpad], axis=1)
        assert pw % (2 * LANE) == 0

        proj, tail = _inproj(h.reshape(bsz * t, d), w16, 4 * gw, w_tail, tm=1024, tn=1024)
        proj = proj.reshape(bsz, t, 4 * gw)
        tail = tail.reshape(bsz, t, pw + 2 * LANE)
        beta, gc = _gates(tail, _pad_lanes(a_log[l]), _pad_lanes(dt_bias[l]), pw // (2 * LANE))
        o_gdn = _gdn(proj, beta, gc, conv_w[l], row2(gdn_norm_w[l]), heads, hb=2, cg=4)
        o_pool = _pool(tail, pool_w[l].astype(BF16), row2(pool_scale[l]), 0)

        h1 = _outproj(o_gdn.reshape(bsz * t, gw), o_pool.reshape(bsz * t, pw), h.reshape(bsz * t, d),
                      w_out[l].astype(BF16), row2(ln1_g[l]), row2(ln1_b[l]), alpha, tm=1024, rb=256)

        mlen = mem.shape[1]
        w_kv = jnp.concatenate([xk_w[l], xv_w[l]], axis=1).astype(BF16)
        kv = _matmul(mem.reshape(bsz * mlen, d), w_kv, BF16, tm=1024, tn=1024).reshape(bsz, mlen, 2 * d)
        h2 = _xattn(h1.reshape(bsz, t, d), xq_w[l].astype(BF16), kv, xo_w[l].astype(BF16),
                    row2(ln2_g[l]), row2(ln2_b[l]), alpha, tm=512, rb=256)

        h3 = _mlp(h2.reshape(bsz * t, d), w_up[l].astype(BF16), w_down[l].astype(BF16),
                  row2(ln3_g[l]), row2(ln3_b[l]), alpha, tm=1024, tf=512, rb=256)
        h = h3.reshape(bsz, t, d)
    return h
```

```python
import functools

import jax
import jax.numpy as jnp
from jax import lax
from jax.experimental import pallas as pl
from jax.experimental.pallas import tpu as pltpu

F32 = jnp.float32
BF16 = jnp.bfloat16

LANE = 128
GDN_CHUNK = 128
POOL_WINDOWS = (2, 4, 8, 16)
XATTN_HEADS = 4
LN_EPS = 1e-5
NORM_EPS = 1e-6
LOG2E = 1.4426950408889634
VMEM_LIMIT = 60 * 1024 * 1024

_NT = (((1,), (1,)), ((), ()))


def _dot(a, b):
    return jnp.dot(a, b, preferred_element_type=F32)


def _layer_norm(y, g, b):
    mu = jnp.mean(y, axis=-1, keepdims=True)
    yc = y - mu
    var = jnp.mean(yc * yc, axis=-1, keepdims=True)
    return yc * lax.rsqrt(var + LN_EPS) * g + b


def _silu(x):
    return x * jax.nn.sigmoid(x)


def _cast_specs(mats, nsteps, step_of):
    ins, outs, shapes = [], [], []
    for w in mats:
        r, c = w.shape
        assert r % (16 * nsteps) == 0
        spec = pl.BlockSpec((r // nsteps, c), lambda *ids: (step_of(*ids), 0))
        ins.append(spec)
        outs.append(spec)
        shapes.append(jax.ShapeDtypeStruct((r, c), BF16))
    return ins, outs, shapes


def _do_casts(src_refs, dst_refs):
    for s, d in zip(src_refs, dst_refs):
        d[...] = s[...].astype(d.dtype)


def _inproj_kernel(x_ref, wm_ref, wt_ref, om_ref, ot_ref, xb_ref, *, nmain):
    j = pl.program_id(1)

    @pl.when(j == 0)
    def _():
        xb_ref[...] = x_ref[...].astype(BF16)

    @pl.when(j < nmain)
    def _():
        om_ref[...] = _dot(xb_ref[...], wm_ref[...]).astype(om_ref.dtype)

    @pl.when(j == nmain)
    def _():
        ot_ref[...] = _dot(xb_ref[...], wt_ref[...]).astype(ot_ref.dtype)


def _inproj(x, w_all, n_main, w_tail, tm, tn, out_dtype=F32):
    m, k = x.shape
    nt = w_tail.shape[1]
    nmain = n_main // tn
    main_idx = lambda j: jnp.minimum(j, nmain - 1)
    return pl.pallas_call(
        functools.partial(_inproj_kernel, nmain=nmain),
        grid=(m // tm, nmain + 1),
        in_specs=[pl.BlockSpec((tm, k), lambda i, j: (i, 0)),
                  pl.BlockSpec((k, tn), lambda i, j: (0, main_idx(j))),
                  pl.BlockSpec((k, nt), lambda i, j: (0, 0), pipeline_mode=pl.Buffered(1))],
        out_specs=[pl.BlockSpec((tm, tn), lambda i, j: (i, main_idx(j))),
                   pl.BlockSpec((tm, nt), lambda i, j: (i, 0))],
        out_shape=[jax.ShapeDtypeStruct((m, n_main), out_dtype), jax.ShapeDtypeStruct((m, nt), out_dtype)],
        scratch_shapes=[pltpu.VMEM((tm, k), BF16)],
        compiler_params=pltpu.CompilerParams(
            dimension_semantics=("parallel", "arbitrary"), vmem_limit_bytes=VMEM_LIMIT),
        name="inproj",
    )(x, w_all, w_tail)


def _gates_kernel(ba_ref, alog_ref, dtb_ref, beta_ref, gc_ref):
    t = ba_ref.shape[1]
    c = GDN_CHUNK
    beta_ref[0] = jax.nn.sigmoid(ba_ref[0, :, :LANE])
    g = -jnp.exp(alog_ref[...]) * jax.nn.softplus(ba_ref[0, :, LANE:] + dtb_ref[...])
    row = lax.broadcasted_iota(jnp.int32, (c, c), 0)
    col = lax.broadcasted_iota(jnp.int32, (c, c), 1)
    tri = (row >= col).astype(F32)
    for i in range(t // c):
        gc_ref[0, i * c:(i + 1) * c, :] = LOG2E * jnp.dot(
            tri, g[i * c:(i + 1) * c], precision=lax.Precision.HIGHEST, preferred_element_type=F32)


def _gates(proj, alog_pad, dtb_pad, ba_block):
    b, t, _ = proj.shape
    return pl.pallas_call(
        _gates_kernel,
        grid=(b,),
        in_specs=[pl.BlockSpec((1, t, 2 * LANE), lambda i: (i, 0, ba_block)),
                  pl.BlockSpec((1, LANE), lambda i: (0, 0)),
                  pl.BlockSpec((1, LANE), lambda i: (0, 0))],
        out_specs=[pl.BlockSpec((1, t, LANE), lambda i: (i, 0, 0)),
                   pl.BlockSpec((1, t, LANE), lambda i: (i, 0, 0))],
        out_shape=[jax.ShapeDtypeStruct((b, t, LANE), F32)] * 2,
        compiler_params=pltpu.CompilerParams(dimension_semantics=("parallel",)),
        name="gdn_gates",
    )(proj, alog_pad, dtb_pad)


def _gdn_kernel(q_ref, k_ref, v_ref, z_ref, beta_ref, gc_ref, cwq_ref, cwk_ref, cwv_ref, nw_ref,
                o_ref, pad_ref, u_s, wq_s, ak_s, dl_s, *, hb, cg, scale):
    t = q_ref.shape[1]
    c = GDN_CHUNK
    nc = t // c
    conv_k = cwq_ref.shape[0]
    halo = 8
    jh = pl.program_id(1)

    for slot, src in enumerate((q_ref, k_ref, v_ref)):
        for s in range(hb):
            pad_ref[slot * hb + s, 0:halo, :] = jnp.zeros((halo, LANE), F32)
            pad_ref[slot * hb + s, halo:halo + t, :] = src[0, :, s * LANE:(s + 1) * LANE]

    row = lax.broadcasted_iota(jnp.int32, (c, c), 0)
    col = lax.broadcasted_iota(jnp.int32, (c, c), 1)
    incl = row >= col
    strict = row > col
    eye = (row == col).astype(F32)
    lane = lax.broadcasted_iota(jnp.int32, (c, LANE), 1)

    n_sq = (c - 1).bit_length() - 1
    n_prep = hb * cg * (4 + n_sq)
    n_seq = cg * (2 + hb)

    def prep_group(g, sb):
        st = []
        for s, cc in [(s, cc) for s in range(hb) for cc in range(cg)]:
            ci = g * cg + cc
            r0 = pl.multiple_of(ci * c, c)
            sl = slice(s * LANE, (s + 1) * LANE)

            def conv_act(slot, cw_ref):
                slab = slot * hb + s
                acc = cw_ref[0:1, sl] * pad_ref[slab, pl.ds(r0 + halo - conv_k + 1, c), :]
                for j in range(1, conv_k):
                    acc = acc + cw_ref[j:j + 1, sl] * pad_ref[slab, pl.ds(r0 + halo - conv_k + 1 + j, c), :]
                return _silu(acc)

            q = conv_act(0, cwq_ref)
            k = conv_act(1, cwk_ref)
            v = conv_act(2, cwv_ref)
            q = q * lax.rsqrt(jnp.sum(q * q, axis=-1, keepdims=True) + NORM_EPS) * scale
            k = k * lax.rsqrt(jnp.sum(k * k, axis=-1, keepdims=True) + NORM_EPS)

            hm = lane == (jh * hb + s)
            bcol = jnp.sum(jnp.where(hm, beta_ref[0, pl.ds(r0, c), :], 0.0), axis=1, keepdims=True)
            gcol = jnp.sum(jnp.where(hm, gc_ref[0, pl.ds(r0, c), :], 0.0), axis=1, keepdims=True)
            gcb = jnp.broadcast_to(gcol, (c, LANE))
            gct = gcb.T
            e = jnp.exp2(jnp.where(incl, gcb - gct, 0.0))
            ecol = jnp.exp2(gcol)
            kb = k * bcol
            glast = gcb[c - 1:c, :]
            dl_s[sb, s, cc] = jnp.broadcast_to(jnp.exp2(glast), (8, LANE))
            st.append(dict(
                s=s, cc=cc, dec=jnp.where(incl, e, 0.0), ndec=jnp.where(strict, -e, 0.0),
                kbq=jnp.concatenate([kb, q], axis=0).astype(BF16), k16=k.astype(BF16),
                rhs=jnp.concatenate([v * bcol, kb * ecol], axis=1).astype(BF16),
                qg=(q * ecol).astype(BF16),
                kdt=(k * jnp.exp2(glast - gcb)).T.astype(BF16)))
            yield

        for d in st:
            kq = lax.dot_general(d["kbq"], d["k16"], _NT, preferred_element_type=F32)
            d["attn"] = (kq[c:] * d["dec"]).astype(BF16)
            d["x"] = kq[:c] * d["ndec"]
            d["p"] = eye + d["x"]
            yield
        for d in st:
            x16 = d["x"].astype(BF16)
            d["x"] = _dot(x16, x16)
            yield
        for n in range(1, n_sq + 1):
            for d in st:
                x16 = d["x"].astype(BF16)
                if n < n_sq:
                    r = _dot(jnp.concatenate([x16, d["p"].astype(BF16)], axis=0), x16)
                    d["x"] = r[:c]
                    d["p"] = d["p"] + r[c:]
                else:
                    d["p"] = d["p"] + _dot(d["p"].astype(BF16), x16)
                yield
        for d in st:
            s, cc = d["s"], d["cc"]
            uw = _dot(d["p"].astype(BF16), d["rhs"])
            u_s[sb, s, cc] = uw[:, :LANE]
            wq_s[sb, s, cc] = jnp.concatenate([uw[:, LANE:].astype(BF16), d["qg"]], axis=0)
            ak_s[sb, s, cc] = jnp.concatenate([d["attn"], d["kdt"]], axis=0)
            yield

    def seq_group(g, sb, states):
        for cc in range(cg):
            r0 = pl.multiple_of((g * cg + cc) * c, c)
            rs = [_dot(wq_s[sb, s, cc], states[s].astype(BF16)) for s in range(hb)]
            yield
            vns = [(u_s[sb, s, cc] - rs[s][:c]).astype(BF16) for s in range(hb)]
            r2s = [_dot(ak_s[sb, s, cc], vns[s]) for s in range(hb)]
            yield
            for s in range(hb):
                sl = slice(s * LANE, (s + 1) * LANE)
                states[s] = states[s] * dl_s[sb, s, cc][0:1, :] + r2s[s][c:]
                o = rs[s][c:] + r2s[s][:c]
                z = z_ref[0, pl.ds(r0, c), sl]
                o = o * lax.rsqrt(jnp.mean(o * o, axis=-1, keepdims=True) + NORM_EPS) * nw_ref[...]
                o_ref[0, pl.ds(r0, c), sl] = (o * _silu(z)).astype(o_ref.dtype)
                yield

    def weave(seq, prep):
        done = 0
        for i in range(n_prep):
            while done * n_prep <= i * n_seq and done < n_seq:
                next(seq)
                done += 1
            next(prep)
        for _ in range(done, n_seq):
            next(seq)

    u_s[1] = jnp.zeros(u_s.shape[1:], u_s.dtype)
    wq_s[1] = jnp.zeros(wq_s.shape[1:], wq_s.dtype)
    ak_s[1] = jnp.zeros(ak_s.shape[1:], ak_s.dtype)
    dl_s[1] = jnp.zeros(dl_s.shape[1:], dl_s.dtype)
    ng = nc // cg

    def pipe_body(m, states):
        states = list(states)
        weave(seq_group(jnp.maximum(2 * m - 1, 0), 1, states), prep_group(2 * m, 0))
        weave(seq_group(2 * m, 0, states), prep_group(2 * m + 1, 1))
        return tuple(states)

    states = lax.fori_loop(0, ng // 2, pipe_body, tuple(jnp.zeros((LANE, LANE), F32) for _ in range(hb)))
    for _ in seq_group(ng - 1, 1, list(states)):
        pass


def _gdn(proj, beta, gc, conv_w, norm_w, heads, hb, cg):
    b, t, _ = proj.shape
    nj = heads // hb
    assert (t // GDN_CHUNK) % (2 * cg) == 0 and heads % hb == 0
    wblk = hb * LANE
    kern = functools.partial(_gdn_kernel, hb=hb, cg=cg, scale=float(LANE) ** -0.5)
    return pl.pallas_call(
        kern,
        grid=(b, nj),
        in_specs=[pl.BlockSpec((1, t, wblk), lambda i, j: (i, 0, j)),
                  pl.BlockSpec((1, t, wblk), lambda i, j: (i, 0, nj + j)),
                  pl.BlockSpec((1, t, wblk), lambda i, j: (i, 0, 2 * nj + j)),
                  pl.BlockSpec((1, t, wblk), lambda i, j: (i, 0, 3 * nj + j)),
                  pl.BlockSpec((1, t, LANE), lambda i, j: (i, 0, 0)),
                  pl.BlockSpec((1, t, LANE), lambda i, j: (i, 0, 0)),
                  pl.BlockSpec((conv_w.shape[0], wblk), lambda i, j: (0, j)),
                  pl.BlockSpec((conv_w.shape[0], wblk), lambda i, j: (0, nj + j)),
                  pl.BlockSpec((conv_w.shape[0], wblk), lambda i, j: (0, 2 * nj + j)),
                  pl.BlockSpec((1, LANE), lambda i, j: (0, 0))],
        out_specs=pl.BlockSpec((1, t, wblk), lambda i, j: (i, 0, j)),
        out_shape=jax.ShapeDtypeStruct((b, t, heads * LANE), BF16),
        scratch_shapes=[pltpu.VMEM((3 * hb, t + 8, LANE), F32),
                        pltpu.VMEM((2, hb, cg, GDN_CHUNK, LANE), F32),
                        pltpu.VMEM((2, hb, cg, 2 * GDN_CHUNK, LANE), BF16),
                        pltpu.VMEM((2, hb, cg, 2 * GDN_CHUNK, LANE), BF16),
                        pltpu.VMEM((2, hb, cg, 8, LANE), F32)],
        compiler_params=pltpu.CompilerParams(
            dimension_semantics=("parallel", "arbitrary"), vmem_limit_bytes=VMEM_LIMIT),
        name="gdn_delta_rule",
    )(proj, proj, proj, proj, beta, gc, conv_w, conv_w, conv_w, norm_w)


def _pool_kernel(*refs, rows, ncast):
    p_ref, w_ref, sc_ref = refs[:3]
    o_ref, pad_ref = refs[3 + ncast], refs[-1]
    _do_casts(refs[3:3 + ncast], refs[4 + ncast:-1])
    t = p_ref.shape[1]
    cgd = w_ref.shape[1]
    halo = 16
    nslab = cgd // LANE
    for g, win in enumerate(POOL_WINDOWS):
        sl = slice(g * cgd, (g + 1) * cgd)
        for k in range(nslab):
            pad_ref[k, 0:halo, :] = jnp.zeros((halo, LANE), F32)
            pad_ref[k, halo:halo + t, :] = p_ref[0, :, g * cgd + k * LANE:g * cgd + (k + 1) * LANE]

        def body(i, carry, win=win, sl=sl, g=g):
            r0 = pl.multiple_of(i * rows, rows)

            def shifted(d):
                return jnp.concatenate(
                    [pad_ref[k, pl.ds(r0 + halo - d, rows), :] for k in range(nslab)], axis=1)

            x = shifted(0)
            ssum = x
            for d in range(1, win):
                ssum = ssum + shifted(d)
            pos = r0 + lax.broadcasted_iota(jnp.int32, (rows, cgd), 0)
            cnt = jnp.minimum(pos + 1, win).astype(F32)
            pooled = ssum / cnt - x
            mixed = _dot(pooled.astype(BF16), w_ref[g]) * sc_ref[:, sl]
            o_ref[0, pl.ds(r0, rows), sl] = mixed.astype(o_ref.dtype)
            return carry

        lax.fori_loop(0, t // rows, body, 0)


def _pool(proj, pool_w16, pool_scale, p_block, casts):
    b, t, _ = proj.shape
    g, cgd, _ = pool_w16.shape
    width = g * cgd
    cin, cout, cshape = _cast_specs(casts, b, lambda i: i)
    return pl.pallas_call(
        functools.partial(_pool_kernel, rows=256, ncast=len(casts)),
        grid=(b,),
        in_specs=[pl.BlockSpec((1, t, width), lambda i: (i, 0, p_block)),
                  pl.BlockSpec((g, cgd, cgd), lambda i: (0, 0, 0)),
                  pl.BlockSpec((1, width), lambda i: (0, 0))] + cin,
        out_specs=[pl.BlockSpec((1, t, width), lambda i: (i, 0, 0))] + cout,
        out_shape=[jax.ShapeDtypeStruct((b, t, width), BF16)] + cshape,
        scratch_shapes=[pltpu.VMEM((cgd // LANE, t + 16, LANE), F32)],
        compiler_params=pltpu.CompilerParams(
            dimension_semantics=("parallel",), vmem_limit_bytes=VMEM_LIMIT),
        name="multiscale_pool",
    )(proj, pool_w16, pool_scale, *casts)


def _outproj_kernel(*refs, alpha, rb, ncast):
    og_ref, op_ref, x_ref, wa_ref, wb_ref, g_ref, b_ref = refs[:7]
    o_ref = refs[7 + ncast]
    _do_casts(refs[7:7 + ncast], refs[8 + ncast:])
    for r in range(o_ref.shape[0] // rb):
        rows = slice(r * rb, (r + 1) * rb)
        mix = _dot(og_ref[rows, :], wa_ref[...]) + _dot(op_ref[rows, :], wb_ref[...])
        o_ref[rows, :] = _layer_norm(alpha * x_ref[rows, :] + mix, g_ref[...], b_ref[...])


def _outproj(og, op, x, w, g, b, alpha, tm, rb, casts):
    m, d = x.shape
    ka, kb = og.shape[1], op.shape[1]
    assert ka % kb == 0 and w.shape[0] == ka + kb
    const = lambda i: (0, 0)
    once = pl.Buffered(1)
    cin, cout, cshape = _cast_specs(casts, m // tm, lambda i: i)
    return pl.pallas_call(
        functools.partial(_outproj_kernel, alpha=alpha, rb=rb, ncast=len(casts)),
        grid=(m // tm,),
        in_specs=[pl.BlockSpec((tm, ka), lambda i: (i, 0)),
                  pl.BlockSpec((tm, kb), lambda i: (i, 0)),
                  pl.BlockSpec((tm, d), lambda i: (i, 0)),
                  pl.BlockSpec((ka, d), const, pipeline_mode=once),
                  pl.BlockSpec((kb, d), lambda i: (ka // kb, 0), pipeline_mode=once),
                  pl.BlockSpec((1, d), const),
                  pl.BlockSpec((1, d), const)] + cin,
        out_specs=[pl.BlockSpec((tm, d), lambda i: (i, 0))] + cout,
        out_shape=[jax.ShapeDtypeStruct((m, d), F32)] + cshape,
        compiler_params=pltpu.CompilerParams(
            dimension_semantics=("parallel",), vmem_limit_bytes=VMEM_LIMIT),
        name="outproj_ln",
    )(og, op, x, w, w, g, b, *casts)


def _xattn_kernel(*refs, alpha, heads, rb, ncast):
    h_ref, wq_ref, k_ref, v_ref, wo_ref, g_ref, b_ref = refs[:7]
    o_ref = refs[7 + ncast]
    _do_casts(refs[7:7 + ncast], refs[8 + ncast:])
    d = h_ref.shape[-1]
    hd = d // heads
    for r in range(h_ref.shape[1] // rb):
        rows = slice(r * rb, (r + 1) * rb)
        h = h_ref[0, rows, :]
        q = _dot(h.astype(BF16), wq_ref[...])
        outs = []
        for i in range(heads):
            qh = q[:, i * hd:(i + 1) * hd].astype(BF16)
            kh = k_ref[0, :, i * hd:(i + 1) * hd]
            vh = v_ref[0, :, i * hd:(i + 1) * hd]
            s = lax.dot_general(qh, kh, _NT, preferred_element_type=F32) * (float(hd) ** -0.5)
            e = jnp.exp(s - jnp.max(s, axis=-1, keepdims=True))
            p = e / jnp.sum(e, axis=-1, keepdims=True)
            outs.append(_dot(p.astype(BF16), vh).astype(BF16))
        xa = _dot(jnp.concatenate(outs, axis=1), wo_ref[...])
        o_ref[0, rows, :] = _layer_norm(alpha * h + xa, g_ref[...], b_ref[...])


def _xattn(h, wq, k, v, wo, g, b, alpha, tm, rb, casts):
    bsz, t, d = h.shape
    mlen = k.shape[1]
    nj = t // tm
    const = lambda i, j: (0, 0)
    once = pl.Buffered(1)
    cin, cout, cshape = _cast_specs(casts, bsz * nj, lambda i, j: i * nj + j)
    return pl.pallas_call(
        functools.partial(_xattn_kernel, alpha=alpha, heads=XATTN_HEADS, rb=rb, ncast=len(casts)),
        grid=(bsz, nj),
        in_specs=[pl.BlockSpec((1, tm, d), lambda i, j: (i, j, 0)),
                  pl.BlockSpec((d, d), const, pipeline_mode=once),
                  pl.BlockSpec((1, mlen, d), lambda i, j: (i, 0, 0)),
                  pl.BlockSpec((1, mlen, d), lambda i, j: (i, 0, 0)),
                  pl.BlockSpec((d, d), const, pipeline_mode=once),
                  pl.BlockSpec((1, d), const),
                  pl.BlockSpec((1, d), const)] + cin,
        out_specs=[pl.BlockSpec((1, tm, d), lambda i, j: (i, j, 0))] + cout,
        out_shape=[jax.ShapeDtypeStruct((bsz, t, d), F32)] + cshape,
        compiler_params=pltpu.CompilerParams(
            dimension_semantics=("parallel", "parallel"), vmem_limit_bytes=VMEM_LIMIT),
        name="xattn_ln",
    )(h, wq, k, v, wo, g, b, *casts)


def _mlp_kernel(h_ref, wu_ref, wd_ref, g_ref, b_ref, o_ref, *, alpha, rb):
    j = pl.program_id(1)

    @pl.when(j == 0)
    def _():
        o_ref[...] = jnp.zeros_like(o_ref)

    u = jnp.maximum(_dot(h_ref[...].astype(BF16), wu_ref[...]), 0.0)
    o_ref[...] += _dot((u * u).astype(BF16), wd_ref[...])

    @pl.when(j == pl.num_programs(1) - 1)
    def _():
        for r in range(o_ref.shape[0] // rb):
            rows = slice(r * rb, (r + 1) * rb)
            o_ref[rows, :] = _layer_norm(alpha * h_ref[rows, :] + o_ref[rows, :], g_ref[...], b_ref[...])


def _mlp(h, wu, wd, g, b, alpha, tm, tf, rb):
    m, d = h.shape
    f = wu.shape[1]
    const = lambda i, j: (0, 0)
    return pl.pallas_call(
        functools.partial(_mlp_kernel, alpha=alpha, rb=rb),
        grid=(m // tm, f // tf),
        in_specs=[pl.BlockSpec((tm, d), lambda i, j: (i, 0)),
                  pl.BlockSpec((d, tf), lambda i, j: (0, j)),
                  pl.BlockSpec((tf, d), lambda i, j: (j, 0)),
                  pl.BlockSpec((1, d), const),
                  pl.BlockSpec((1, d), const)],
        out_specs=pl.BlockSpec((tm, d), lambda i, j: (i, 0)),
        out_shape=jax.ShapeDtypeStruct((m, d), F32),
        compiler_params=pltpu.CompilerParams(
            dimension_semantics=("parallel", "arbitrary"), vmem_limit_bytes=VMEM_LIMIT),
        name="mlp_ln",
    )(h, wu, wd, g, b)


def _pad_lanes(v):
    return jnp.pad(v.astype(F32), (0, LANE - v.shape[0]))[None, :]


def kernel(x, mem, w_in, conv_w, a_log, dt_bias, gdn_norm_w, pool_w, pool_scale, w_out, ln1_g, ln1_b,
           xq_w, xk_w, xv_w, xo_w, ln2_g, ln2_b, w_up, w_down, ln3_g, ln3_b):
    bsz, t, d = x.shape
    depth = w_in.shape[0]
    heads = a_log.shape[1]
    gw = conv_w.shape[2] // 3
    pw = pool_w.shape[1] * pool_w.shape[2]
    assert gw == heads * LANE and t % GDN_CHUNK == 0 and heads <= LANE
    alpha = (2.0 * depth) ** 0.25
    row2 = lambda v: v[None, :]

    h = x
    for l in range(depth):
        w16 = w_in[l].astype(BF16)
        zpad = jnp.zeros((d, LANE - heads), BF16)
        w_tail = jnp.concatenate(
            [w16[:, 4 * gw + 2 * heads:], w16[:, 4 * gw:4 * gw + heads], zpad,
             w16[:, 4 * gw + heads:4 * gw + 2 * heads], zpad], axis=1)
        assert pw % (2 * LANE) == 0

        proj, tail = _inproj(h.reshape(bsz * t, d), w16, 4 * gw, w_tail, tm=1024, tn=1024)
        proj = proj.reshape(bsz, t, 4 * gw)
        tail = tail.reshape(bsz, t, pw + 2 * LANE)
        beta, gc = _gates(tail, _pad_lanes(a_log[l]), _pad_lanes(dt_bias[l]), pw // (2 * LANE))
        o_gdn = _gdn(proj, beta, gc, conv_w[l], row2(gdn_norm_w[l]), heads, hb=2, cg=4)
        o_pool, wo16, xk16, xv16 = _pool(tail, pool_w[l].astype(BF16), row2(pool_scale[l]), 0,
                                         casts=(w_out[l], xk_w[l], xv_w[l]))

        h1, xq16, xo16 = _outproj(o_gdn.reshape(bsz * t, gw), o_pool.reshape(bsz * t, pw),
                                  h.reshape(bsz * t, d), wo16, row2(ln1_g[l]), row2(ln1_b[l]), alpha,
                                  tm=1024, rb=256, casts=(xq_w[l], xo_w[l]))

        mlen = mem.shape[1]
        mk, mv = _inproj(mem.reshape(bsz * mlen, d), xk16, d, xv16, tm=1024, tn=1024, out_dtype=BF16)
        h2, wu16, wd16 = _xattn(h1.reshape(bsz, t, d), xq16, mk.reshape(bsz, mlen, d), mv.reshape(bsz, mlen, d),
                                xo16, row2(ln2_g[l]), row2(ln2_b[l]), alpha, tm=512, rb=256,
                                casts=(w_up[l], w_down[l]))

        h3 = _mlp(h2.reshape(bsz * t, d), wu16, wd16,
                  row2(ln3_g[l]), row2(ln3_b[l]), alpha, tm=1024, tf=1024, rb=256)
        h = h3.reshape(bsz, t, d)
    return h
```

```python
import functools

import jax
import jax.numpy as jnp
from jax import lax
from jax.experimental import pallas as pl
from jax.experimental.pallas import tpu as pltpu

F32 = jnp.float32
BF16 = jnp.bfloat16

LANE = 128
GDN_CHUNK = 128
POOL_WINDOWS = (2, 4, 8, 16)
XATTN_HEADS = 4
LN_EPS = 1e-5
NORM_EPS = 1e-6
LOG2E = 1.4426950408889634
VMEM_LIMIT = 60 * 1024 * 1024

_NT = (((1,), (1,)), ((), ()))


def _dot(a, b):
    return jnp.dot(a, b, preferred_element_type=F32)


def _layer_norm(y, g, b):
    mu = jnp.mean(y, axis=-1, keepdims=True)
    yc = y - mu
    var = jnp.mean(yc * yc, axis=-1, keepdims=True)
    return yc * lax.rsqrt(var + LN_EPS) * g + b


def _silu(x):
    return x * jax.nn.sigmoid(x)


def _cast_specs(mats, nsteps, step_of):
    ins, outs, shapes = [], [], []
    for w in mats:
        r, c = w.shape
        assert r % (16 * nsteps) == 0
        spec = pl.BlockSpec((r // nsteps, c), lambda *ids: (step_of(*ids), 0))
        ins.append(spec)
        outs.append(spec)
        shapes.append(jax.ShapeDtypeStruct((r, c), BF16))
    return ins, outs, shapes


def _do_casts(src_refs, dst_refs):
    for s, d in zip(src_refs, dst_refs):
        d[...] = s[...].astype(d.dtype)


def _inproj_kernel(x_ref, wm_ref, wt_ref, om_ref, ot_ref, xb_ref, *, nmain):
    j = pl.program_id(1)

    @pl.when(j == 0)
    def _():
        xb_ref[...] = x_ref[...].astype(BF16)

    @pl.when(j < nmain)
    def _():
        om_ref[...] = _dot(xb_ref[...], wm_ref[...]).astype(om_ref.dtype)

    @pl.when(j == nmain)
    def _():
        ot_ref[...] = _dot(xb_ref[...], wt_ref[...]).astype(ot_ref.dtype)


def _inproj(x, w_all, n_main, w_tail, tm, tn, out_dtype=F32):
    m, k = x.shape
    nt = w_tail.shape[1]
    nmain = n_main // tn
    main_idx = lambda j: jnp.minimum(j, nmain - 1)
    return pl.pallas_call(
        functools.partial(_inproj_kernel, nmain=nmain),
        grid=(m // tm, nmain + 1),
        in_specs=[pl.BlockSpec((tm, k), lambda i, j: (i, 0)),
                  pl.BlockSpec((k, tn), lambda i, j: (0, main_idx(j))),
                  pl.BlockSpec((k, nt), lambda i, j: (0, 0), pipeline_mode=pl.Buffered(1))],
        out_specs=[pl.BlockSpec((tm, tn), lambda i, j: (i, main_idx(j))),
                   pl.BlockSpec((tm, nt), lambda i, j: (i, 0))],
        out_shape=[jax.ShapeDtypeStruct((m, n_main), out_dtype), jax.ShapeDtypeStruct((m, nt), out_dtype)],
        scratch_shapes=[pltpu.VMEM((tm, k), BF16)],
        compiler_params=pltpu.CompilerParams(
            dimension_semantics=("parallel", "arbitrary"), vmem_limit_bytes=VMEM_LIMIT),
        name="inproj",
    )(x, w_all, w_tail)


def _gates_kernel(ba_ref, alog_ref, dtb_ref, beta_ref, gc_ref):
    t = ba_ref.shape[1]
    c = GDN_CHUNK
    beta_ref[0] = jax.nn.sigmoid(ba_ref[0, :, :LANE])
    g = -jnp.exp(alog_ref[...]) * jax.nn.softplus(ba_ref[0, :, LANE:] + dtb_ref[...])
    row = lax.broadcasted_iota(jnp.int32, (c, c), 0)
    col = lax.broadcasted_iota(jnp.int32, (c, c), 1)
    tri = (row >= col).astype(F32)
    for i in range(t // c):
        gc_ref[0, i * c:(i + 1) * c, :] = LOG2E * jnp.dot(
            tri, g[i * c:(i + 1) * c], precision=lax.Precision.HIGHEST, preferred_element_type=F32)


def _gates(proj, alog_pad, dtb_pad, ba_block):
    b, t, _ = proj.shape
    return pl.pallas_call(
        _gates_kernel,
        grid=(b,),
        in_specs=[pl.BlockSpec((1, t, 2 * LANE), lambda i: (i, 0, ba_block)),
                  pl.BlockSpec((1, LANE), lambda i: (0, 0)),
                  pl.BlockSpec((1, LANE), lambda i: (0, 0))],
        out_specs=[pl.BlockSpec((1, t, LANE), lambda i: (i, 0, 0)),
                   pl.BlockSpec((1, t, LANE), lambda i: (i, 0, 0))],
        out_shape=[jax.ShapeDtypeStruct((b, t, LANE), F32)] * 2,
        compiler_params=pltpu.CompilerParams(dimension_semantics=("parallel",)),
        name="gdn_gates",
    )(proj, alog_pad, dtb_pad)


def _gdn_kernel(q_ref, k_ref, v_ref, z_ref, beta_ref, gc_ref, cwq_ref, cwk_ref, cwv_ref, nw_ref,
                o_ref, pad_ref, u_s, wq_s, ak_s, dl_s, *, hb, cg, scale):
    t = q_ref.shape[1]
    c = GDN_CHUNK
    nc = t // c
    conv_k = cwq_ref.shape[0]
    halo = 8
    jh = pl.program_id(1)

    for slot, src in enumerate((q_ref, k_ref, v_ref)):
        for s in range(hb):
            pad_ref[slot * hb + s, 0:halo, :] = jnp.zeros((halo, LANE), F32)
            pad_ref[slot * hb + s, halo:halo + t, :] = src[0, :, s * LANE:(s + 1) * LANE]

    row = lax.broadcasted_iota(jnp.int32, (c, c), 0)
    col = lax.broadcasted_iota(jnp.int32, (c, c), 1)
    incl = row >= col
    strict = row > col
    eye = (row == col).astype(F32)
    lane = lax.broadcasted_iota(jnp.int32, (c, LANE), 1)

    n_sq = (c - 1).bit_length() - 1
    n_prep = hb * cg * (4 + n_sq)
    n_seq = cg * (2 + hb)

    def prep_group(g, sb):
        st = []
        for s, cc in [(s, cc) for s in range(hb) for cc in range(cg)]:
            ci = g * cg + cc
            r0 = pl.multiple_of(ci * c, c)
            sl = slice(s * LANE, (s + 1) * LANE)

            def conv_act(slot, cw_ref):
                slab = slot * hb + s
                acc = cw_ref[0:1, sl] * pad_ref[slab, pl.ds(r0 + halo - conv_k + 1, c), :]
                for j in range(1, conv_k):
                    acc = acc + cw_ref[j:j + 1, sl] * pad_ref[slab, pl.ds(r0 + halo - conv_k + 1 + j, c), :]
                return _silu(acc)

            q = conv_act(0, cwq_ref)
            k = conv_act(1, cwk_ref)
            v = conv_act(2, cwv_ref)
            q = q * lax.rsqrt(jnp.sum(q * q, axis=-1, keepdims=True) + NORM_EPS) * scale
            k = k * lax.rsqrt(jnp.sum(k * k, axis=-1, keepdims=True) + NORM_EPS)

            hm = lane == (jh * hb + s)
            bcol = jnp.sum(jnp.where(hm, beta_ref[0, pl.ds(r0, c), :], 0.0), axis=1, keepdims=True)
            gcol = jnp.sum(jnp.where(hm, gc_ref[0, pl.ds(r0, c), :], 0.0), axis=1, keepdims=True)
            gcb = jnp.broadcast_to(gcol, (c, LANE))
            gct = gcb.T
            e = jnp.exp2(jnp.where(incl, gcb - gct, 0.0))
            ecol = jnp.exp2(gcol)
            kb = k * bcol
            glast = gcb[c - 1:c, :]
            dl_s[sb, s, cc] = jnp.broadcast_to(jnp.exp2(glast), (8, LANE))
            st.append(dict(
                s=s, cc=cc, dec=jnp.where(incl, e, 0.0), ndec=jnp.where(strict, -e, 0.0),
                kbq=jnp.concatenate([kb, q], axis=0).astype(BF16), k16=k.astype(BF16),
                rhs=jnp.concatenate([v * bcol, kb * ecol], axis=1).astype(BF16),
                qg=(q * ecol).astype(BF16),
                kdt=(k * jnp.exp2(glast - gcb)).T.astype(BF16)))
            yield

        for d in st:
            kq = lax.dot_general(d["kbq"], d["k16"], _NT, preferred_element_type=F32)
            d["attn"] = (kq[c:] * d["dec"]).astype(BF16)
            d["x"] = kq[:c] * d["ndec"]
            d["p"] = eye + d["x"]
            yield
        for d in st:
            x16 = d["x"].astype(BF16)
            d["x"] = _dot(x16, x16)
            yield
        for n in range(1, n_sq + 1):
            for d in st:
                x16 = d["x"].astype(BF16)
                if n < n_sq:
                    r = _dot(jnp.concatenate([x16, d["p"].astype(BF16)], axis=0), x16)
                    d["x"] = r[:c]
                    d["p"] = d["p"] + r[c:]
                else:
                    d["p"] = d["p"] + _dot(d["p"].astype(BF16), x16)
                yield
        for d in st:
            s, cc = d["s"], d["cc"]
            uw = _dot(d["p"].astype(BF16), d["rhs"])
            u_s[sb, s, cc] = uw[:, :LANE]
            wq_s[sb, s, cc] = jnp.concatenate([uw[:, LANE:].astype(BF16), d["qg"]], axis=0)
            ak_s[sb, s, cc] = jnp.concatenate([d["attn"], d["kdt"]], axis=0)
            yield

    def seq_group(g, sb, states):
        for cc in range(cg):
            r0 = pl.multiple_of((g * cg + cc) * c, c)
            rs = [_dot(wq_s[sb, s, cc], states[s].astype(BF16)) for s in range(hb)]
            yield
            vns = [(u_s[sb, s, cc] - rs[s][:c]).astype(BF16) for s in range(hb)]
            r2s = [_dot(ak_s[sb, s, cc], vns[s]) for s in range(hb)]
            yield
            for s in range(hb):
                sl = slice(s * LANE, (s + 1) * LANE)
                states[s] = states[s] * dl_s[sb, s, cc][0:1, :] + r2s[s][c:]
                o = rs[s][c:] + r2s[s][:c]
                z = z_ref[0, pl.ds(r0, c), sl]
                o = o * lax.rsqrt(jnp.mean(o * o, axis=-1, keepdims=True) + NORM_EPS) * nw_ref[...]
                o_ref[0, pl.ds(r0, c), sl] = (o * _silu(z)).astype(o_ref.dtype)
                yield

    def weave(seq, prep):
        done = 0
        for i in range(n_prep):
            while done * n_prep <= i * n_seq and done < n_seq:
                next(seq)
                done += 1
            next(prep)
        for _ in range(done, n_seq):
            next(seq)

    u_s[1] = jnp.zeros(u_s.shape[1:], u_s.dtype)
    wq_s[1] = jnp.zeros(wq_s.shape[1:], wq_s.dtype)
    ak_s[1] = jnp.zeros(ak_s.shape[1:], ak_s.dtype)
    dl_s[1] = jnp.zeros(dl_s.shape[1:], dl_s.dtype)
    ng = nc // cg

    def pipe_body(m, states):
        states = list(states)
        weave(seq_group(jnp.maximum(2 * m - 1, 0), 1, states), prep_group(2 * m, 0))
        weave(seq_group(2 * m, 0, states), prep_group(2 * m + 1, 1))
        return tuple(states)

    states = lax.fori_loop(0, ng // 2, pipe_body, tuple(jnp.zeros((LANE, LANE), F32) for _ in range(hb)))
    for _ in seq_group(ng - 1, 1, list(states)):
        pass


def _gdn(proj, beta, gc, conv_w, norm_w, heads, hb, cg):
    b, t, _ = proj.shape
    nj = heads // hb
    assert (t // GDN_CHUNK) % (2 * cg) == 0 and heads % hb == 0
    wblk = hb * LANE
    kern = functools.partial(_gdn_kernel, hb=hb, cg=cg, scale=float(LANE) ** -0.5)
    return pl.pallas_call(
        kern,
        grid=(b, nj),
        in_specs=[pl.BlockSpec((1, t, wblk), lambda i, j: (i, 0, j)),
                  pl.BlockSpec((1, t, wblk), lambda i, j: (i, 0, nj + j)),
                  pl.BlockSpec((1, t, wblk), lambda i, j: (i, 0, 2 * nj + j)),
                  pl.BlockSpec((1, t, wblk), lambda i, j: (i, 0, 3 * nj + j)),
                  pl.BlockSpec((1, t, LANE), lambda i, j: (i, 0, 0)),
                  pl.BlockSpec((1, t, LANE), lambda i, j: (i, 0, 0)),
                  pl.BlockSpec((conv_w.shape[0], wblk), lambda i, j: (0, j)),
                  pl.BlockSpec((conv_w.shape[0], wblk), lambda i, j: (0, nj + j)),
                  pl.BlockSpec((conv_w.shape[0], wblk), lambda i, j: (0, 2 * nj + j)),
                  pl.BlockSpec((1, LANE), lambda i, j: (0, 0))],
        out_specs=pl.BlockSpec((1, t, wblk), lambda i, j: (i, 0, j)),
        out_shape=jax.ShapeDtypeStruct((b, t, heads * LANE), BF16),
        scratch_shapes=[pltpu.VMEM((3 * hb, t + 8, LANE), F32),
                        pltpu.VMEM((2, hb, cg, GDN_CHUNK, LANE), F32),
                        pltpu.VMEM((2, hb, cg, 2 * GDN_CHUNK, LANE), BF16),
                        pltpu.VMEM((2, hb, cg, 2 * GDN_CHUNK, LANE), BF16),
                        pltpu.VMEM((2, hb, cg, 8, LANE), F32)],
        compiler_params=pltpu.CompilerParams(
            dimension_semantics=("parallel", "arbitrary"), vmem_limit_bytes=VMEM_LIMIT),
        name="gdn_delta_rule",
    )(proj, proj, proj, proj, beta, gc, conv_w, conv_w, conv_w, norm_w)


def _pool_kernel(*refs, rows, ncast):
    p_ref, w_ref, sc_ref = refs[:3]
    o_ref, pad_ref = refs[3 + ncast], refs[-1]
    _do_casts(refs[3:3 + ncast], refs[4 + ncast:-1])
    t = p_ref.shape[1]
    cgd = w_ref.shape[1]
    halo = 16
    nslab = cgd // LANE
    for g, win in enumerate(POOL_WINDOWS):
        sl = slice(g * cgd, (g + 1) * cgd)
        for k in range(nslab):
            pad_ref[k, 0:halo, :] = jnp.zeros((halo, LANE), F32)
            pad_ref[k, halo:halo + t, :] = p_ref[0, :, g * cgd + k * LANE:g * cgd + (k + 1) * LANE]

        def body(i, carry, win=win, sl=sl, g=g):
            r0 = pl.multiple_of(i * rows, rows)

            def shifted(d):
                return jnp.concatenate(
                    [pad_ref[k, pl.ds(r0 + halo - d, rows), :] for k in range(nslab)], axis=1)

            x = shifted(0)
            ssum = x
            for d in range(1, win):
                ssum = ssum + shifted(d)
            pos = r0 + lax.broadcasted_iota(jnp.int32, (rows, cgd), 0)
            cnt = jnp.minimum(pos + 1, win).astype(F32)
            pooled = ssum / cnt - x
            mixed = _dot(pooled.astype(BF16), w_ref[g]) * sc_ref[:, sl]
            o_ref[0, pl.ds(r0, rows), sl] = mixed.astype(o_ref.dtype)
            return carry

        lax.fori_loop(0, t // rows, body, 0)


def _pool(proj, pool_w16, pool_scale, p_block, casts):
    b, t, _ = proj.shape
    g, cgd, _ = pool_w16.shape
    width = g * cgd
    cin, cout, cshape = _cast_specs(casts, b, lambda i: i)
    return pl.pallas_call(
        functools.partial(_pool_kernel, rows=256, ncast=len(casts)),
        grid=(b,),
        in_specs=[pl.BlockSpec((1, t, width), lambda i: (i, 0, p_block)),
                  pl.BlockSpec((g, cgd, cgd), lambda i: (0, 0, 0)),
                  pl.BlockSpec((1, width), lambda i: (0, 0))] + cin,
        out_specs=[pl.BlockSpec((1, t, width), lambda i: (i, 0, 0))] + cout,
        out_shape=[jax.ShapeDtypeStruct((b, t, width), BF16)] + cshape,
        scratch_shapes=[pltpu.VMEM((cgd // LANE, t + 16, LANE), F32)],
        compiler_params=pltpu.CompilerParams(
            dimension_semantics=("parallel",), vmem_limit_bytes=VMEM_LIMIT),
        name="multiscale_pool",
    )(proj, pool_w16, pool_scale, *casts)


def _outproj_kernel(*refs, alpha, rb, ncast):
    og_ref, op_ref, x_ref, wa_ref, wb_ref, g_ref, b_ref = refs[:7]
    o_ref = refs[7 + ncast]
    _do_casts(refs[7:7 + ncast], refs[8 + ncast:])
    for r in range(o_ref.shape[0] // rb):
        rows = slice(r * rb, (r + 1) * rb)
        mix = _dot(og_ref[rows, :], wa_ref[...]) + _dot(op_ref[rows, :], wb_ref[...])
        o_ref[rows, :] = _layer_norm(alpha * x_ref[rows, :] + mix, g_ref[...], b_ref[...])


def _outproj(og, op, x, w, g, b, alpha, tm, rb, casts):
    m, d = x.shape
    ka, kb = og.shape[1], op.shape[1]
    assert ka % kb == 0 and w.shape[0] == ka + kb
    const = lambda i: (0, 0)
    once = pl.Buffered(1)
    cin, cout, cshape = _cast_specs(casts, m // tm, lambda i: i)
    return pl.pallas_call(
        functools.partial(_outproj_kernel, alpha=alpha, rb=rb, ncast=len(casts)),
        grid=(m // tm,),
        in_specs=[pl.BlockSpec((tm, ka), lambda i: (i, 0)),
                  pl.BlockSpec((tm, kb), lambda i: (i, 0)),
                  pl.BlockSpec((tm, d), lambda i: (i, 0)),
                  pl.BlockSpec((ka, d), const, pipeline_mode=once),
                  pl.BlockSpec((kb, d), lambda i: (ka // kb, 0), pipeline_mode=once),
                  pl.BlockSpec((1, d), const),
                  pl.BlockSpec((1, d), const)] + cin,
        out_specs=[pl.BlockSpec((tm, d), lambda i: (i, 0))] + cout,
        out_shape=[jax.ShapeDtypeStruct((m, d), F32)] + cshape,
        compiler_params=pltpu.CompilerParams(
            dimension_semantics=("parallel",), vmem_limit_bytes=VMEM_LIMIT),
        name="outproj_ln",
    )(og, op, x, w, w, g, b, *casts)


def _xattn_kernel(*refs, alpha, heads, rb, ncast):
    h_ref, wq_ref, k_ref, v_ref, wo_ref, g_ref, b_ref = refs[:7]
    o_ref = refs[7 + ncast]
    _do_casts(refs[7:7 + ncast], refs[8 + ncast:])
    d = h_ref.shape[-1]
    hd = d // heads
    for r in range(h_ref.shape[1] // rb):
        rows = slice(r * rb, (r + 1) * rb)
        h = h_ref[0, rows, :]
        q = _dot(h.astype(BF16), wq_ref[...])
        outs = []
        for i in range(heads):
            qh = q[:, i * hd:(i + 1) * hd].astype(BF16)
            kh = k_ref[0, :, i * hd:(i + 1) * hd]
            vh = v_ref[0, :, i * hd:(i + 1) * hd]
            s = lax.dot_general(qh, kh, _NT, preferred_element_type=F32) * (float(hd) ** -0.5)
            e = jnp.exp(s - jnp.max(s, axis=-1, keepdims=True))
            p = e / jnp.sum(e, axis=-1, keepdims=True)
            outs.append(_dot(p.astype(BF16), vh).astype(BF16))
        xa = _dot(jnp.concatenate(outs, axis=1), wo_ref[...])
        o_ref[0, rows, :] = _layer_norm(alpha * h + xa, g_ref[...], b_ref[...])


def _xattn(h, wq, k, v, wo, g, b, alpha, tm, rb, casts):
    bsz, t, d = h.shape
    mlen = k.shape[1]
    nj = t // tm
    const = lambda i, j: (0, 0)
    once = pl.Buffered(1)
    cin, cout, cshape = _cast_specs(casts, bsz * nj, lambda i, j: i * nj + j)
    return pl.pallas_call(
        functools.partial(_xattn_kernel, alpha=alpha, heads=XATTN_HEADS, rb=rb, ncast=len(casts)),
        grid=(bsz, nj),
        in_specs=[pl.BlockSpec((1, tm, d), lambda i, j: (i, j, 0)),
                  pl.BlockSpec((d, d), const, pipeline_mode=once),
                  pl.BlockSpec((1, mlen, d), lambda i, j: (i, 0, 0)),
                  pl.BlockSpec((1, mlen, d), lambda i, j: (i, 0, 0)),
                  pl.BlockSpec((d, d), const, pipeline_mode=once),
                  pl.BlockSpec((1, d), const),
                  pl.BlockSpec((1, d), const)] + cin,
        out_specs=[pl.BlockSpec((1, tm, d), lambda i, j: (i, j, 0))] + cout,
        out_shape=[jax.ShapeDtypeStruct((bsz, t, d), F32)] + cshape,
        compiler_params=pltpu.CompilerParams(
            dimension_semantics=("parallel", "parallel"), vmem_limit_bytes=VMEM_LIMIT),
        name="xattn_ln",
    )(h, wq, k, v, wo, g, b, *casts)


def _mlp_kernel(h_ref, wu_ref, wd_ref, g_ref, b_ref, o_ref, *, alpha, rb):
    j = pl.program_id(1)

    @pl.when(j == 0)
    def _():
        o_ref[...] = jnp.zeros_like(o_ref)

    def ff(rows):
        u = jnp.maximum(_dot(h_ref[rows, :].astype(BF16), wu_ref[...]), 0.0)
        return _dot((u * u).astype(BF16), wd_ref[...])

    last = pl.num_programs(1) - 1

    @pl.when(j < last)
    def _():
        o_ref[...] += ff(slice(None))

    @pl.when(j == last)
    def _():
        for r in range(o_ref.shape[0] // rb):
            rows = slice(r * rb, (r + 1) * rb)
            y = alpha * h_ref[rows, :] + (o_ref[rows, :] + ff(rows))
            o_ref[rows, :] = _layer_norm(y, g_ref[...], b_ref[...])


def _mlp(h, wu, wd, g, b, alpha, tm, tf, rb):
    m, d = h.shape
    f = wu.shape[1]
    const = lambda i, j: (0, 0)
    return pl.pallas_call(
        functools.partial(_mlp_kernel, alpha=alpha, rb=rb),
        grid=(m // tm, f // tf),
        in_specs=[pl.BlockSpec((tm, d), lambda i, j: (i, 0)),
                  pl.BlockSpec((d, tf), lambda i, j: (0, j)),
                  pl.BlockSpec((tf, d), lambda i, j: (j, 0)),
                  pl.BlockSpec((1, d), const),
                  pl.BlockSpec((1, d), const)],
        out_specs=pl.BlockSpec((tm, d), lambda i, j: (i, 0)),
        out_shape=jax.ShapeDtypeStruct((m, d), F32),
        compiler_params=pltpu.CompilerParams(
            dimension_semantics=("parallel", "arbitrary"), vmem_limit_bytes=VMEM_LIMIT),
        name="mlp_ln",
    )(h, wu, wd, g, b)


def _pad_lanes(v):
    return jnp.pad(v.astype(F32), (0, LANE - v.shape[0]))[None, :]


def kernel(x, mem, w_in, conv_w, a_log, dt_bias, gdn_norm_w, pool_w, pool_scale, w_out, ln1_g, ln1_b,
           xq_w, xk_w, xv_w, xo_w, ln2_g, ln2_b, w_up, w_down, ln3_g, ln3_b):
    bsz, t, d = x.shape
    depth = w_in.shape[0]
    heads = a_log.shape[1]
    gw = conv_w.shape[2] // 3
    pw = pool_w.shape[1] * pool_w.shape[2]
    assert gw == heads * LANE and t % GDN_CHUNK == 0 and heads <= LANE
    alpha = (2.0 * depth) ** 0.25
    row2 = lambda v: v[None, :]

    h = x
    for l in range(depth):
        w16 = w_in[l].astype(BF16)
        zpad = jnp.zeros((d, LANE - heads), BF16)
        w_tail = jnp.concatenate(
            [w16[:, 4 * gw + 2 * heads:], w16[:, 4 * gw:4 * gw + heads], zpad,
             w16[:, 4 * gw + heads:4 * gw + 2 * heads], zpad], axis=1)
        assert pw % (2 * LANE) == 0

        proj, tail = _inproj(h.reshape(bsz * t, d), w16, 4 * gw, w_tail, tm=1024, tn=1024)
        proj = proj.reshape(bsz, t, 4 * gw)
        tail = tail.reshape(bsz, t, pw + 2 * LANE)
        beta, gc = _gates(tail, _pad_lanes(a_log[l]), _pad_lanes(dt_bias[l]), pw // (2 * LANE))
        o_gdn = _gdn(proj, beta, gc, conv_w[l], row2(gdn_norm_w[l]), heads, hb=2, cg=4)
        o_pool, wo16, xk16, xv16 = _pool(tail, pool_w[l].astype(BF16), row2(pool_scale[l]), 0,
                                         casts=(w_out[l], xk_w[l], xv_w[l]))

        h1, xq16, xo16 = _outproj(o_gdn.reshape(bsz * t, gw), o_pool.reshape(bsz * t, pw),
                                  h.reshape(bsz * t, d), wo16, row2(ln1_g[l]), row2(ln1_b[l]), alpha,
                                  tm=1024, rb=256, casts=(xq_w[l], xo_w[l]))

        mlen = mem.shape[1]
        mk, mv = _inproj(mem.reshape(bsz * mlen, d), xk16, d, xv16, tm=1024, tn=1024, out_dtype=BF16)
        h2, wu16, wd16 = _xattn(h1.reshape(bsz, t, d), xq16, mk.reshape(bsz, mlen, d), mv.reshape(bsz, mlen, d),
                                xo16, row2(ln2_g[l]), row2(ln2_b[l]), alpha, tm=512, rb=256,
                                casts=(w_up[l], w_down[l]))

        h3 = _mlp(h2.reshape(bsz * t, d), wu16, wd16,
                  row2(ln3_g[l]), row2(ln3_b[l]), alpha, tm=1024, tf=1024, rb=256)
        h = h3.reshape(bsz, t, d)
    return h
```

```python
import functools

import jax
import jax.numpy as jnp
from jax import lax
from jax.experimental import pallas as pl
from jax.experimental.pallas import tpu as pltpu

F32 = jnp.float32
BF16 = jnp.bfloat16

LANE = 128
GDN_CHUNK = 128
POOL_WINDOWS = (2, 4, 8, 16)
XATTN_HEADS = 4
LN_EPS = 1e-5
NORM_EPS = 1e-6
LOG2E = 1.4426950408889634
VMEM_LIMIT = 60 * 1024 * 1024

_NT = (((1,), (1,)), ((), ()))


def _dot(a, b):
    return jnp.dot(a, b, preferred_element_type=F32)


def _layer_norm(y, g, b):
    mu = jnp.mean(y, axis=-1, keepdims=True)
    yc = y - mu
    var = jnp.mean(yc * yc, axis=-1, keepdims=True)
    return yc * lax.rsqrt(var + LN_EPS) * g + b


def _silu(x):
    return x * jax.nn.sigmoid(x)


def _cast_specs(mats, nsteps, step_of):
    ins, outs, shapes = [], [], []
    for w in mats:
        r, c = w.shape
        assert r % (16 * nsteps) == 0
        spec = pl.BlockSpec((r // nsteps, c), lambda *ids: (step_of(*ids), 0))
        ins.append(spec)
        outs.append(spec)
        shapes.append(jax.ShapeDtypeStruct((r, c), BF16))
    return ins, outs, shapes


def _do_casts(src_refs, dst_refs):
    for s, d in zip(src_refs, dst_refs):
        d[...] = s[...].astype(d.dtype)


def _inproj_kernel(x_ref, wm_ref, wt_ref, om_ref, ot_ref, xb_ref, *, nmain):
    j = pl.program_id(1)

    @pl.when(j == 0)
    def _():
        xb_ref[...] = x_ref[...].astype(BF16)

    @pl.when(j < nmain)
    def _():
        om_ref[...] = _dot(xb_ref[...], wm_ref[...]).astype(om_ref.dtype)

    @pl.when(j == nmain)
    def _():
        ot_ref[...] = _dot(xb_ref[...], wt_ref[...]).astype(ot_ref.dtype)


def _inproj(x, w_all, n_main, w_tail, tm, tn, out_dtype=F32, sectioned=False):
    m, k = x.shape
    nt = w_tail.shape[1]
    nmain = n_main // tn
    main_idx = lambda j: jnp.minimum(j, nmain - 1)
    if sectioned:
        main_spec = pl.BlockSpec((None, tm, tn), lambda i, j: (main_idx(j), i, 0))
        main_shape = jax.ShapeDtypeStruct((nmain, m, tn), out_dtype)
    else:
        main_spec = pl.BlockSpec((tm, tn), lambda i, j: (i, main_idx(j)))
        main_shape = jax.ShapeDtypeStruct((m, n_main), out_dtype)
    return pl.pallas_call(
        functools.partial(_inproj_kernel, nmain=nmain),
        grid=(m // tm, nmain + 1),
        in_specs=[pl.BlockSpec((tm, k), lambda i, j: (i, 0)),
                  pl.BlockSpec((k, tn), lambda i, j: (0, main_idx(j))),
                  pl.BlockSpec((k, nt), lambda i, j: (0, 0), pipeline_mode=pl.Buffered(1))],
        out_specs=[main_spec, pl.BlockSpec((tm, nt), lambda i, j: (i, 0))],
        out_shape=[main_shape, jax.ShapeDtypeStruct((m, nt), out_dtype)],
        scratch_shapes=[pltpu.VMEM((tm, k), BF16)],
        compiler_params=pltpu.CompilerParams(
            dimension_semantics=("parallel", "arbitrary"), vmem_limit_bytes=VMEM_LIMIT),
        name="inproj",
    )(x, w_all, w_tail)


def _gates_kernel(ba_ref, alog_ref, dtb_ref, beta_ref, gc_ref):
    t = ba_ref.shape[1]
    c = GDN_CHUNK
    beta_ref[0] = jax.nn.sigmoid(ba_ref[0, :, :LANE])
    g = -jnp.exp(alog_ref[...]) * jax.nn.softplus(ba_ref[0, :, LANE:] + dtb_ref[...])
    row = lax.broadcasted_iota(jnp.int32, (c, c), 0)
    col = lax.broadcasted_iota(jnp.int32, (c, c), 1)
    tri = (row >= col).astype(F32)
    for i in range(t // c):
        gc_ref[0, i * c:(i + 1) * c, :] = LOG2E * jnp.dot(
            tri, g[i * c:(i + 1) * c], precision=lax.Precision.HIGHEST, preferred_element_type=F32)


def _gates(proj, alog_pad, dtb_pad, ba_block):
    b, t, _ = proj.shape
    return pl.pallas_call(
        _gates_kernel,
        grid=(b,),
        in_specs=[pl.BlockSpec((1, t, 2 * LANE), lambda i: (i, 0, ba_block)),
                  pl.BlockSpec((1, LANE), lambda i: (0, 0)),
                  pl.BlockSpec((1, LANE), lambda i: (0, 0))],
        out_specs=[pl.BlockSpec((1, t, LANE), lambda i: (i, 0, 0)),
                   pl.BlockSpec((1, t, LANE), lambda i: (i, 0, 0))],
        out_shape=[jax.ShapeDtypeStruct((b, t, LANE), F32)] * 2,
        compiler_params=pltpu.CompilerParams(dimension_semantics=("parallel",)),
        name="gdn_gates",
    )(proj, alog_pad, dtb_pad)


def _gdn_kernel(q_ref, k_ref, v_ref, z_ref, beta_ref, gc_ref, cwq_ref, cwk_ref, cwv_ref, nw_ref,
                o_ref, pad_ref, u_s, wq_s, ak_s, dl_s, *, hb, cg, scale):
    t = q_ref.shape[1]
    c = GDN_CHUNK
    nc = t // c
    conv_k = cwq_ref.shape[0]
    halo = 8
    jh = pl.program_id(1)

    for slot, src in enumerate((q_ref, k_ref, v_ref)):
        for s in range(hb):
            pad_ref[slot * hb + s, 0:halo, :] = jnp.zeros((halo, LANE), F32)
            pad_ref[slot * hb + s, halo:halo + t, :] = src[0, :, s * LANE:(s + 1) * LANE]

    row = lax.broadcasted_iota(jnp.int32, (c, c), 0)
    col = lax.broadcasted_iota(jnp.int32, (c, c), 1)
    incl = row >= col
    strict = row > col
    eye = (row == col).astype(F32)
    lane = lax.broadcasted_iota(jnp.int32, (c, LANE), 1)

    n_sq = (c - 1).bit_length() - 1
    n_prep = hb * cg * (4 + n_sq)
    n_seq = cg * (2 + hb)

    def prep_group(g, sb):
        st = []
        for s, cc in [(s, cc) for s in range(hb) for cc in range(cg)]:
            ci = g * cg + cc
            r0 = pl.multiple_of(ci * c, c)
            sl = slice(s * LANE, (s + 1) * LANE)

            def conv_act(slot, cw_ref):
                slab = slot * hb + s
                acc = cw_ref[0:1, sl] * pad_ref[slab, pl.ds(r0 + halo - conv_k + 1, c), :]
                for j in range(1, conv_k):
                    acc = acc + cw_ref[j:j + 1, sl] * pad_ref[slab, pl.ds(r0 + halo - conv_k + 1 + j, c), :]
                return _silu(acc)

            q = conv_act(0, cwq_ref)
            k = conv_act(1, cwk_ref)
            v = conv_act(2, cwv_ref)
            q = q * lax.rsqrt(jnp.sum(q * q, axis=-1, keepdims=True) + NORM_EPS) * scale
            k = k * lax.rsqrt(jnp.sum(k * k, axis=-1, keepdims=True) + NORM_EPS)

            hm = lane == (jh * hb + s)
            bcol = jnp.sum(jnp.where(hm, beta_ref[0, pl.ds(r0, c), :], 0.0), axis=1, keepdims=True)
            gcol = jnp.sum(jnp.where(hm, gc_ref[0, pl.ds(r0, c), :], 0.0), axis=1, keepdims=True)
            gcb = jnp.broadcast_to(gcol, (c, LANE))
            gct = gcb.T
            e = jnp.exp2(jnp.where(incl, gcb - gct, 0.0))
            ecol = jnp.exp2(gcol)
            kb = k * bcol
            glast = gcb[c - 1:c, :]
            dl_s[sb, s, cc] = jnp.broadcast_to(jnp.exp2(glast), (8, LANE))
            st.append(dict(
                s=s, cc=cc, dec=jnp.where(incl, e, 0.0), ndec=jnp.where(strict, -e, 0.0),
                kbq=jnp.concatenate([kb, q], axis=0).astype(BF16), k16=k.astype(BF16),
                rhs=jnp.concatenate([v * bcol, kb * ecol], axis=1).astype(BF16),
                qg=(q * ecol).astype(BF16),
                kdt=(k * jnp.exp2(glast - gcb)).T.astype(BF16)))
            yield

        for d in st:
            kq = lax.dot_general(d["kbq"], d["k16"], _NT, preferred_element_type=F32)
            d["attn"] = (kq[c:] * d["dec"]).astype(BF16)
            d["x"] = kq[:c] * d["ndec"]
            d["p"] = eye + d["x"]
            yield
        for d in st:
            x16 = d["x"].astype(BF16)
            d["x"] = _dot(x16, x16)
            yield
        for n in range(1, n_sq + 1):
            for d in st:
                x16 = d["x"].astype(BF16)
                if n < n_sq:
                    r = _dot(jnp.concatenate([x16, d["p"].astype(BF16)], axis=0), x16)
                    d["x"] = r[:c]
                    d["p"] = d["p"] + r[c:]
                else:
                    d["p"] = d["p"] + _dot(d["p"].astype(BF16), x16)
                yield
        for d in st:
            s, cc = d["s"], d["cc"]
            uw = _dot(d["p"].astype(BF16), d["rhs"])
            u_s[sb, s, cc] = uw[:, :LANE]
            wq_s[sb, s, cc] = jnp.concatenate([uw[:, LANE:].astype(BF16), d["qg"]], axis=0)
            ak_s[sb, s, cc] = jnp.concatenate([d["attn"], d["kdt"]], axis=0)
            yield

    def seq_group(g, sb, states):
        for cc in range(cg):
            r0 = pl.multiple_of((g * cg + cc) * c, c)
            rs = [_dot(wq_s[sb, s, cc], states[s].astype(BF16)) for s in range(hb)]
            yield
            vns = [(u_s[sb, s, cc] - rs[s][:c]).astype(BF16) for s in range(hb)]
            r2s = [_dot(ak_s[sb, s, cc], vns[s]) for s in range(hb)]
            yield
            for s in range(hb):
                sl = slice(s * LANE, (s + 1) * LANE)
                states[s] = states[s] * dl_s[sb, s, cc][0:1, :] + r2s[s][c:]
                o = rs[s][c:] + r2s[s][:c]
                z = z_ref[0, pl.ds(r0, c), sl]
                o = o * lax.rsqrt(jnp.mean(o * o, axis=-1, keepdims=True) + NORM_EPS) * nw_ref[...]
                o_ref[0, pl.ds(r0, c), sl] = (o * _silu(z)).astype(o_ref.dtype)
                yield

    def weave(seq, prep):
        done = 0
        for i in range(n_prep):
            while done * n_prep <= i * n_seq and done < n_seq:
                next(seq)
                done += 1
            next(prep)
        for _ in range(done, n_seq):
            next(seq)

    u_s[1] = jnp.zeros(u_s.shape[1:], u_s.dtype)
    wq_s[1] = jnp.zeros(wq_s.shape[1:], wq_s.dtype)
    ak_s[1] = jnp.zeros(ak_s.shape[1:], ak_s.dtype)
    dl_s[1] = jnp.zeros(dl_s.shape[1:], dl_s.dtype)
    ng = nc // cg

    def pipe_body(m, states):
        states = list(states)
        weave(seq_group(jnp.maximum(2 * m - 1, 0), 1, states), prep_group(2 * m, 0))
        weave(seq_group(2 * m, 0, states), prep_group(2 * m + 1, 1))
        return tuple(states)

    states = lax.fori_loop(0, ng // 2, pipe_body, tuple(jnp.zeros((LANE, LANE), F32) for _ in range(hb)))
    for _ in seq_group(ng - 1, 1, list(states)):
        pass


def _gdn(proj, beta, gc, conv_w, norm_w, heads, hb, cg):
    _, b, t, _ = proj.shape
    nj = heads // hb
    assert (t // GDN_CHUNK) % (2 * cg) == 0 and heads % hb == 0
    wblk = hb * LANE
    kern = functools.partial(_gdn_kernel, hb=hb, cg=cg, scale=float(LANE) ** -0.5)
    section = lambda n: pl.BlockSpec((None, 1, t, wblk), lambda i, j: (n, i, 0, j))
    return pl.pallas_call(
        kern,
        grid=(b, nj),
        in_specs=[section(0), section(1), section(2), section(3),
                  pl.BlockSpec((1, t, LANE), lambda i, j: (i, 0, 0)),
                  pl.BlockSpec((1, t, LANE), lambda i, j: (i, 0, 0)),
                  pl.BlockSpec((conv_w.shape[0], wblk), lambda i, j: (0, j)),
                  pl.BlockSpec((conv_w.shape[0], wblk), lambda i, j: (0, nj + j)),
                  pl.BlockSpec((conv_w.shape[0], wblk), lambda i, j: (0, 2 * nj + j)),
                  pl.BlockSpec((1, LANE), lambda i, j: (0, 0))],
        out_specs=pl.BlockSpec((1, t, wblk), lambda i, j: (i, 0, j)),
        out_shape=jax.ShapeDtypeStruct((b, t, heads * LANE), BF16),
        scratch_shapes=[pltpu.VMEM((3 * hb, t + 8, LANE), F32),
                        pltpu.VMEM((2, hb, cg, GDN_CHUNK, LANE), F32),
                        pltpu.VMEM((2, hb, cg, 2 * GDN_CHUNK, LANE), BF16),
                        pltpu.VMEM((2, hb, cg, 2 * GDN_CHUNK, LANE), BF16),
                        pltpu.VMEM((2, hb, cg, 8, LANE), F32)],
        compiler_params=pltpu.CompilerParams(
            dimension_semantics=("parallel", "arbitrary"), vmem_limit_bytes=VMEM_LIMIT),
        name="gdn_delta_rule",
    )(proj, proj, proj, proj, beta, gc, conv_w, conv_w, conv_w, norm_w)


def _pool_kernel(*refs, rows, ncast):
    p_ref, w_ref, sc_ref = refs[:3]
    o_ref, pad_ref = refs[3 + ncast], refs[-1]
    _do_casts(refs[3:3 + ncast], refs[4 + ncast:-1])
    t = p_ref.shape[1]
    cgd = w_ref.shape[1]
    halo = 16
    nslab = cgd // LANE
    for g, win in enumerate(POOL_WINDOWS):
        sl = slice(g * cgd, (g + 1) * cgd)
        for k in range(nslab):
            pad_ref[k, 0:halo, :] = jnp.zeros((halo, LANE), F32)
            pad_ref[k, halo:halo + t, :] = p_ref[0, :, g * cgd + k * LANE:g * cgd + (k + 1) * LANE]

        def body(i, carry, win=win, sl=sl, g=g):
            r0 = pl.multiple_of(i * rows, rows)

            def shifted(d):
                return jnp.concatenate(
                    [pad_ref[k, pl.ds(r0 + halo - d, rows), :] for k in range(nslab)], axis=1)

            x = shifted(0)
            ssum = x
            for d in range(1, win):
                ssum = ssum + shifted(d)
            pos = r0 + lax.broadcasted_iota(jnp.int32, (rows, cgd), 0)
            cnt = jnp.minimum(pos + 1, win).astype(F32)
            pooled = ssum / cnt - x
            mixed = _dot(pooled.astype(BF16), w_ref[g]) * sc_ref[:, sl]
            o_ref[0, pl.ds(r0, rows), sl] = mixed.astype(o_ref.dtype)
            return carry

        lax.fori_loop(0, t // rows, body, 0)


def _pool(proj, pool_w16, pool_scale, p_block, casts):
    b, t, _ = proj.shape
    g, cgd, _ = pool_w16.shape
    width = g * cgd
    cin, cout, cshape = _cast_specs(casts, b, lambda i: i)
    return pl.pallas_call(
        functools.partial(_pool_kernel, rows=256, ncast=len(casts)),
        grid=(b,),
        in_specs=[pl.BlockSpec((1, t, width), lambda i: (i, 0, p_block)),
                  pl.BlockSpec((g, cgd, cgd), lambda i: (0, 0, 0)),
                  pl.BlockSpec((1, width), lambda i: (0, 0))] + cin,
        out_specs=[pl.BlockSpec((1, t, width), lambda i: (i, 0, 0))] + cout,
        out_shape=[jax.ShapeDtypeStruct((b, t, width), BF16)] + cshape,
        scratch_shapes=[pltpu.VMEM((cgd // LANE, t + 16, LANE), F32)],
        compiler_params=pltpu.CompilerParams(
            dimension_semantics=("parallel",), vmem_limit_bytes=VMEM_LIMIT),
        name="multiscale_pool",
    )(proj, pool_w16, pool_scale, *casts)


def _outproj_kernel(*refs, alpha, rb, ncast):
    og_ref, op_ref, x_ref, wa_ref, wb_ref, g_ref, b_ref = refs[:7]
    o_ref = refs[7 + ncast]
    _do_casts(refs[7:7 + ncast], refs[8 + ncast:])
    for r in range(o_ref.shape[0] // rb):
        rows = slice(r * rb, (r + 1) * rb)
        mix = _dot(og_ref[rows, :], wa_ref[...]) + _dot(op_ref[rows, :], wb_ref[...])
        o_ref[rows, :] = _layer_norm(alpha * x_ref[rows, :] + mix, g_ref[...], b_ref[...])


def _outproj(og, op, x, w, g, b, alpha, tm, rb, casts):
    m, d = x.shape
    ka, kb = og.shape[1], op.shape[1]
    assert ka % kb == 0 and w.shape[0] == ka + kb
    const = lambda i: (0, 0)
    once = pl.Buffered(1)
    cin, cout, cshape = _cast_specs(casts, m // tm, lambda i: i)
    return pl.pallas_call(
        functools.partial(_outproj_kernel, alpha=alpha, rb=rb, ncast=len(casts)),
        grid=(m // tm,),
        in_specs=[pl.BlockSpec((tm, ka), lambda i: (i, 0)),
                  pl.BlockSpec((tm, kb), lambda i: (i, 0)),
                  pl.BlockSpec((tm, d), lambda i: (i, 0)),
                  pl.BlockSpec((ka, d), const, pipeline_mode=once),
                  pl.BlockSpec((kb, d), lambda i: (ka // kb, 0), pipeline_mode=once),
                  pl.BlockSpec((1, d), const),
                  pl.BlockSpec((1, d), const)] + cin,
        out_specs=[pl.BlockSpec((tm, d), lambda i: (i, 0))] + cout,
        out_shape=[jax.ShapeDtypeStruct((m, d), F32)] + cshape,
        compiler_params=pltpu.CompilerParams(
            dimension_semantics=("parallel",), vmem_limit_bytes=VMEM_LIMIT),
        name="outproj_ln",
    )(og, op, x, w, w, g, b, *casts)


def _xattn_kernel(*refs, alpha, heads, rb, ncast):
    h_ref, wq_ref, k_ref, v_ref, wo_ref, g_ref, b_ref = refs[:7]
    o_ref = refs[7 + ncast]
    _do_casts(refs[7:7 + ncast], refs[8 + ncast:])
    d = h_ref.shape[-1]
    hd = d // heads
    for r in range(h_ref.shape[1] // rb):
        rows = slice(r * rb, (r + 1) * rb)
        h = h_ref[0, rows, :]
        q = _dot(h.astype(BF16), wq_ref[...])
        outs = []
        for i in range(heads):
            qh = q[:, i * hd:(i + 1) * hd].astype(BF16)
            kh = k_ref[0, :, i * hd:(i + 1) * hd]
            vh = v_ref[0, :, i * hd:(i + 1) * hd]
            s = lax.dot_general(qh, kh, _NT, preferred_element_type=F32) * (float(hd) ** -0.5)
            e = jnp.exp(s - jnp.max(s, axis=-1, keepdims=True))
            p = e / jnp.sum(e, axis=-1, keepdims=True)
            outs.append(_dot(p.astype(BF16), vh).astype(BF16))
        xa = _dot(jnp.concatenate(outs, axis=1), wo_ref[...])
        o_ref[0, rows, :] = _layer_norm(alpha * h + xa, g_ref[...], b_ref[...])


def _xattn(h, wq, k, v, wo, g, b, alpha, tm, rb, casts):
    bsz, t, d = h.shape
    mlen = k.shape[1]
    nj = t // tm
    const = lambda i, j: (0, 0)
    once = pl.Buffered(1)
    cin, cout, cshape = _cast_specs(casts, bsz * nj, lambda i, j: i * nj + j)
    return pl.pallas_call(
        functools.partial(_xattn_kernel, alpha=alpha, heads=XATTN_HEADS, rb=rb, ncast=len(casts)),
        grid=(bsz, nj),
        in_specs=[pl.BlockSpec((1, tm, d), lambda i, j: (i, j, 0)),
                  pl.BlockSpec((d, d), const, pipeline_mode=once),
                  pl.BlockSpec((1, mlen, d), lambda i, j: (i, 0, 0)),
                  pl.BlockSpec((1, mlen, d), lambda i, j: (i, 0, 0)),
                  pl.BlockSpec((d, d), const, pipeline_mode=once),
                  pl.BlockSpec((1, d), const),
                  pl.BlockSpec((1, d), const)] + cin,
        out_specs=[pl.BlockSpec((1, tm, d), lambda i, j: (i, j, 0))] + cout,
        out_shape=[jax.ShapeDtypeStruct((bsz, t, d), F32)] + cshape,
        compiler_params=pltpu.CompilerParams(
            dimension_semantics=("parallel", "parallel"), vmem_limit_bytes=VMEM_LIMIT),
        name="xattn_ln",
    )(h, wq, k, v, wo, g, b, *casts)


def _mlp_kernel(h_ref, wu_ref, wd_ref, g_ref, b_ref, o_ref, *, alpha, rb):
    j = pl.program_id(1)

    @pl.when(j == 0)
    def _():
        o_ref[...] = jnp.zeros_like(o_ref)

    def ff(rows):
        u = jnp.maximum(_dot(h_ref[rows, :].astype(BF16), wu_ref[...]), 0.0)
        return _dot((u * u).astype(BF16), wd_ref[...])

    last = pl.num_programs(1) - 1

    @pl.when(j < last)
    def _():
        o_ref[...] += ff(slice(None))

    @pl.when(j == last)
    def _():
        for r in range(o_ref.shape[0] // rb):
            rows = slice(r * rb, (r + 1) * rb)
            y = alpha * h_ref[rows, :] + (o_ref[rows, :] + ff(rows))
            o_ref[rows, :] = _layer_norm(y, g_ref[...], b_ref[...])


def _mlp(h, wu, wd, g, b, alpha, tm, tf, rb):
    m, d = h.shape
    f = wu.shape[1]
    const = lambda i, j: (0, 0)
    return pl.pallas_call(
        functools.partial(_mlp_kernel, alpha=alpha, rb=rb),
        grid=(m // tm, f // tf),
        in_specs=[pl.BlockSpec((tm, d), lambda i, j: (i, 0)),
                  pl.BlockSpec((d, tf), lambda i, j: (0, j)),
                  pl.BlockSpec((tf, d), lambda i, j: (j, 0)),
                  pl.BlockSpec((1, d), const),
                  pl.BlockSpec((1, d), const)],
        out_specs=pl.BlockSpec((tm, d), lambda i, j: (i, 0)),
        out_shape=jax.ShapeDtypeStruct((m, d), F32),
        compiler_params=pltpu.CompilerParams(
            dimension_semantics=("parallel", "arbitrary"), vmem_limit_bytes=VMEM_LIMIT),
        name="mlp_ln",
    )(h, wu, wd, g, b)


def _pad_lanes(v):
    return jnp.pad(v.astype(F32), (0, LANE - v.shape[0]))[None, :]


def kernel(x, mem, w_in, conv_w, a_log, dt_bias, gdn_norm_w, pool_w, pool_scale, w_out, ln1_g, ln1_b,
           xq_w, xk_w, xv_w, xo_w, ln2_g, ln2_b, w_up, w_down, ln3_g, ln3_b):
    bsz, t, d = x.shape
    depth = w_in.shape[0]
    heads = a_log.shape[1]
    gw = conv_w.shape[2] // 3
    pw = pool_w.shape[1] * pool_w.shape[2]
    assert gw == heads * LANE and t % GDN_CHUNK == 0 and heads <= LANE
    alpha = (2.0 * depth) ** 0.25
    row2 = lambda v: v[None, :]

    h = x
    for l in range(depth):
        w16 = w_in[l].astype(BF16)
        zpad = jnp.zeros((d, LANE - heads), BF16)
        w_tail = jnp.concatenate(
            [w16[:, 4 * gw + 2 * heads:], w16[:, 4 * gw:4 * gw + heads], zpad,
             w16[:, 4 * gw + heads:4 * gw + 2 * heads], zpad], axis=1)
        assert pw % (2 * LANE) == 0

        proj, tail = _inproj(h.reshape(bsz * t, d), w16, 4 * gw, w_tail, tm=1024, tn=gw, sectioned=True)
        proj = proj.reshape(4, bsz, t, gw)
        tail = tail.reshape(bsz, t, pw + 2 * LANE)
        beta, gc = _gates(tail, _pad_lanes(a_log[l]), _pad_lanes(dt_bias[l]), pw // (2 * LANE))
        o_gdn = _gdn(proj, beta, gc, conv_w[l], row2(gdn_norm_w[l]), heads, hb=2, cg=4)
        o_pool, wo16, xk16, xv16 = _pool(tail, pool_w[l].astype(BF16), row2(pool_scale[l]), 0,
                                         casts=(w_out[l], xk_w[l], xv_w[l]))

        h1, xq16, xo16 = _outproj(o_gdn.reshape(bsz * t, gw), o_pool.reshape(bsz * t, pw),
                                  h.reshape(bsz * t, d), wo16, row2(ln1_g[l]), row2(ln1_b[l]), alpha,
                                  tm=1024, rb=256, casts=(xq_w[l], xo_w[l]))

        mlen = mem.shape[1]
        mk, mv = _inproj(mem.reshape(bsz * mlen, d), xk16, d, xv16, tm=1024, tn=1024, out_dtype=BF16)
        h2, wu16, wd16 = _xattn(h1.reshape(bsz, t, d), xq16, mk.reshape(bsz, mlen, d), mv.reshape(bsz, mlen, d),
                                xo16, row2(ln2_g[l]), row2(ln2_b[l]), alpha, tm=512, rb=256,
                                casts=(w_up[l], w_down[l]))

        h3 = _mlp(h2.reshape(bsz * t, d), wu16, wd16,
                  row2(ln3_g[l]), row2(ln3_b[l]), alpha, tm=1024, tf=1024, rb=256)
        h = h3.reshape(bsz, t, d)
    return h
```

```python
import functools

import jax
import jax.numpy as jnp
from jax import lax
from jax.experimental import pallas as pl
from jax.experimental.pallas import tpu as pltpu

F32 = jnp.float32
BF16 = jnp.bfloat16

LANE = 128
GDN_CHUNK = 128
POOL_WINDOWS = (2, 4, 8, 16)
XATTN_HEADS = 4
LN_EPS = 1e-5
NORM_EPS = 1e-6
LOG2E = 1.4426950408889634
SUBLANE = 8
VMEM_LIMIT = 60 * 1024 * 1024
CONV_HALO = SUBLANE
POOL_HALO = 2 * SUBLANE

INPROJ_TM = 1024
KVPROJ_TN = 1024
OUTPROJ_TM, OUTPROJ_RB = 1024, 128
XATTN_TM, XATTN_RB = 512, 256
MLP_TM, MLP_TF, MLP_RB = 1024, 1024, 256
GDN_HEADS_PER_STEP = 2
GDN_CHUNKS_PER_GROUP = 4
POOL_ROWS = 256

_NT = (((1,), (1,)), ((), ()))


def _dot(a, b):
    return jnp.dot(a, b, preferred_element_type=F32)


def _layer_norm(y, g, b):
    mu = jnp.mean(y, axis=-1, keepdims=True)
    yc = y - mu
    var = jnp.mean(yc * yc, axis=-1, keepdims=True)
    return yc * lax.rsqrt(var + LN_EPS) * g + b


def _silu(x):
    return x * jax.nn.sigmoid(x)


def _cast_specs(mats, nsteps, step_of):
    ins, outs, shapes = [], [], []
    for w in mats:
        r, c = w.shape
        assert r % (16 * nsteps) == 0
        spec = pl.BlockSpec((r // nsteps, c), lambda *ids: (step_of(*ids), 0))
        ins.append(spec)
        outs.append(spec)
        shapes.append(jax.ShapeDtypeStruct((r, c), BF16))
    return ins, outs, shapes


def _do_casts(src_refs, dst_refs):
    for s, d in zip(src_refs, dst_refs):
        d[...] = s[...].astype(d.dtype)


def _inproj_kernel(x_ref, wm_ref, wt_ref, om_ref, ot_ref, xb_ref, *, nmain):
    j = pl.program_id(1)

    @pl.when(j == 0)
    def _():
        xb_ref[...] = x_ref[...].astype(BF16)

    @pl.when(j < nmain)
    def _():
        om_ref[...] = _dot(xb_ref[...], wm_ref[...]).astype(om_ref.dtype)

    @pl.when(j == nmain)
    def _():
        ot_ref[...] = _dot(xb_ref[...], wt_ref[...]).astype(ot_ref.dtype)


def _inproj(x, w_all, n_main, w_tail, tm, tn, out_dtype=F32, sectioned=False):
    m, k = x.shape
    nt = w_tail.shape[1]
    nmain = n_main // tn
    main_idx = lambda j: jnp.minimum(j, nmain - 1)
    if sectioned:
        main_spec = pl.BlockSpec((None, tm, tn), lambda i, j: (main_idx(j), i, 0))
        main_shape = jax.ShapeDtypeStruct((nmain, m, tn), out_dtype)
    else:
        main_spec = pl.BlockSpec((tm, tn), lambda i, j: (i, main_idx(j)))
        main_shape = jax.ShapeDtypeStruct((m, n_main), out_dtype)
    return pl.pallas_call(
        functools.partial(_inproj_kernel, nmain=nmain),
        grid=(m // tm, nmain + 1),
        in_specs=[pl.BlockSpec((tm, k), lambda i, j: (i, 0)),
                  pl.BlockSpec((k, tn), lambda i, j: (0, main_idx(j))),
                  pl.BlockSpec((k, nt), lambda i, j: (0, 0), pipeline_mode=pl.Buffered(1))],
        out_specs=[main_spec, pl.BlockSpec((tm, nt), lambda i, j: (i, 0))],
        out_shape=[main_shape, jax.ShapeDtypeStruct((m, nt), out_dtype)],
        scratch_shapes=[pltpu.VMEM((tm, k), BF16)],
        compiler_params=pltpu.CompilerParams(
            dimension_semantics=("parallel", "arbitrary"), vmem_limit_bytes=VMEM_LIMIT),
        name="inproj",
    )(x, w_all, w_tail)


def _gates_kernel(ba_ref, alog_ref, dtb_ref, beta_ref, gc_ref):
    t = ba_ref.shape[1]
    c = GDN_CHUNK
    beta_ref[0] = jax.nn.sigmoid(ba_ref[0, :, :LANE])
    g = -jnp.exp(alog_ref[...]) * jax.nn.softplus(ba_ref[0, :, LANE:] + dtb_ref[...])
    row = lax.broadcasted_iota(jnp.int32, (c, c), 0)
    col = lax.broadcasted_iota(jnp.int32, (c, c), 1)
    tri = (row >= col).astype(F32)
    for i in range(t // c):
        gc_ref[0, i * c:(i + 1) * c, :] = LOG2E * jnp.dot(
            tri, g[i * c:(i + 1) * c], precision=lax.Precision.HIGHEST, preferred_element_type=F32)


def _gates(proj, alog_pad, dtb_pad, ba_block):
    b, t, _ = proj.shape
    return pl.pallas_call(
        _gates_kernel,
        grid=(b,),
        in_specs=[pl.BlockSpec((1, t, 2 * LANE), lambda i: (i, 0, ba_block)),
                  pl.BlockSpec((1, LANE), lambda i: (0, 0)),
                  pl.BlockSpec((1, LANE), lambda i: (0, 0))],
        out_specs=[pl.BlockSpec((1, t, LANE), lambda i: (i, 0, 0)),
                   pl.BlockSpec((1, t, LANE), lambda i: (i, 0, 0))],
        out_shape=[jax.ShapeDtypeStruct((b, t, LANE), F32)] * 2,
        compiler_params=pltpu.CompilerParams(dimension_semantics=("parallel",)),
        name="gdn_gates",
    )(proj, alog_pad, dtb_pad)


def _gdn_kernel(q_ref, k_ref, v_ref, z_ref, beta_ref, gc_ref, cwq_ref, cwk_ref, cwv_ref, nw_ref,
                o_ref, pad_ref, u_s, wq_s, ak_s, dl_s, *, hb, cg, scale):
    t = q_ref.shape[1]
    c = GDN_CHUNK
    nc = t // c
    conv_k = cwq_ref.shape[0]
    halo = CONV_HALO
    assert halo >= conv_k - 1
    jh = pl.program_id(1)

    for slot, src in enumerate((q_ref, k_ref, v_ref)):
        for s in range(hb):
            pad_ref[slot * hb + s, 0:halo, :] = jnp.zeros((halo, LANE), F32)
            pad_ref[slot * hb + s, halo:halo + t, :] = src[0, :, s * LANE:(s + 1) * LANE]

    row = lax.broadcasted_iota(jnp.int32, (c, c), 0)
    col = lax.broadcasted_iota(jnp.int32, (c, c), 1)
    incl = row >= col
    strict = row > col
    eye = (row == col).astype(F32)
    lane = lax.broadcasted_iota(jnp.int32, (c, LANE), 1)

    n_sq = (c - 1).bit_length() - 1
    n_prep = hb * cg * (4 + n_sq)
    n_seq = cg * (2 + hb)

    def prep_group(g, sb):
        st = []
        for s, cc in [(s, cc) for s in range(hb) for cc in range(cg)]:
            ci = g * cg + cc
            r0 = pl.multiple_of(ci * c, c)
            sl = slice(s * LANE, (s + 1) * LANE)

            def conv_act(slot, cw_ref):
                slab = slot * hb + s
                acc = cw_ref[0:1, sl] * pad_ref[slab, pl.ds(r0 + halo - conv_k + 1, c), :]
                for j in range(1, conv_k):
                    acc = acc + cw_ref[j:j + 1, sl] * pad_ref[slab, pl.ds(r0 + halo - conv_k + 1 + j, c), :]
                return _silu(acc)

            q = conv_act(0, cwq_ref)
            k = conv_act(1, cwk_ref)
            v = conv_act(2, cwv_ref)
            q = q * lax.rsqrt(jnp.sum(q * q, axis=-1, keepdims=True) + NORM_EPS) * scale
            k = k * lax.rsqrt(jnp.sum(k * k, axis=-1, keepdims=True) + NORM_EPS)

            hm = lane == (jh * hb + s)
            bcol = jnp.sum(jnp.where(hm, beta_ref[0, pl.ds(r0, c), :], 0.0), axis=1, keepdims=True)
            gcol = jnp.sum(jnp.where(hm, gc_ref[0, pl.ds(r0, c), :], 0.0), axis=1, keepdims=True)
            gcb = jnp.broadcast_to(gcol, (c, LANE))
            gct = gcb.T
            e = jnp.exp2(jnp.where(incl, gcb - gct, 0.0))
            ecol = jnp.exp2(gcol)
            kb = k * bcol
            glast = gcb[c - 1:c, :]
            dl_s[sb, s, cc] = jnp.broadcast_to(jnp.exp2(glast), (SUBLANE, LANE))
            st.append(dict(
                s=s, cc=cc, dec=jnp.where(incl, e, 0.0), ndec=jnp.where(strict, -e, 0.0),
                kbq=jnp.concatenate([kb, q], axis=0).astype(BF16), k16=k.astype(BF16),
                rhs=jnp.concatenate([v * bcol, kb * ecol], axis=1).astype(BF16),
                qg=(q * ecol).astype(BF16),
                kdt=(k * jnp.exp2(glast - gcb)).T.astype(BF16)))
            yield

        for d in st:
            kq = lax.dot_general(d["kbq"], d["k16"], _NT, preferred_element_type=F32)
            d["attn"] = (kq[c:] * d["dec"]).astype(BF16)
            d["x"] = kq[:c] * d["ndec"]
            d["p"] = eye + d["x"]
            yield
        for d in st:
            x16 = d["x"].astype(BF16)
            d["x"] = _dot(x16, x16)
            yield
        for n in range(1, n_sq + 1):
            for d in st:
                x16 = d["x"].astype(BF16)
                if n < n_sq:
                    r = _dot(jnp.concatenate([x16, d["p"].astype(BF16)], axis=0), x16)
                    d["x"] = r[:c]
                    d["p"] = d["p"] + r[c:]
                else:
                    d["p"] = d["p"] + _dot(d["p"].astype(BF16), x16)
                yield
        for d in st:
            s, cc = d["s"], d["cc"]
            uw = _dot(d["p"].astype(BF16), d["rhs"])
            u_s[sb, s, cc] = uw[:, :LANE]
            wq_s[sb, s, cc] = jnp.concatenate([uw[:, LANE:].astype(BF16), d["qg"]], axis=0)
            ak_s[sb, s, cc] = jnp.concatenate([d["attn"], d["kdt"]], axis=0)
            yield

    def seq_group(g, sb, states):
        for cc in range(cg):
            r0 = pl.multiple_of((g * cg + cc) * c, c)
            rs = [_dot(wq_s[sb, s, cc], states[s].astype(BF16)) for s in range(hb)]
            yield
            vns = [(u_s[sb, s, cc] - rs[s][:c]).astype(BF16) for s in range(hb)]
            r2s = [_dot(ak_s[sb, s, cc], vns[s]) for s in range(hb)]
            yield
            for s in range(hb):
                sl = slice(s * LANE, (s + 1) * LANE)
                states[s] = states[s] * dl_s[sb, s, cc][0:1, :] + r2s[s][c:]
                o = rs[s][c:] + r2s[s][:c]
                z = z_ref[0, pl.ds(r0, c), sl]
                o = o * lax.rsqrt(jnp.mean(o * o, axis=-1, keepdims=True) + NORM_EPS) * nw_ref[...]
                o_ref[0, pl.ds(r0, c), sl] = (o * _silu(z)).astype(o_ref.dtype)
                yield

    def weave(seq, prep):
        done = 0
        for i in range(n_prep):
            while done * n_prep <= i * n_seq and done < n_seq:
                next(seq)
                done += 1
            next(prep)
        for _ in range(done, n_seq):
            next(seq)

    u_s[1] = jnp.zeros(u_s.shape[1:], u_s.dtype)
    wq_s[1] = jnp.zeros(wq_s.shape[1:], wq_s.dtype)
    ak_s[1] = jnp.zeros(ak_s.shape[1:], ak_s.dtype)
    dl_s[1] = jnp.zeros(dl_s.shape[1:], dl_s.dtype)
    ng = nc // cg

    def pipe_body(m, states):
        states = list(states)
        weave(seq_group(jnp.maximum(2 * m - 1, 0), 1, states), prep_group(2 * m, 0))
        weave(seq_group(2 * m, 0, states), prep_group(2 * m + 1, 1))
        return tuple(states)

    states = lax.fori_loop(0, ng // 2, pipe_body, tuple(jnp.zeros((LANE, LANE), F32) for _ in range(hb)))
    for _ in seq_group(ng - 1, 1, list(states)):
        pass


def _gdn(proj, beta, gc, conv_w, norm_w, heads, hb, cg):
    _, b, t, _ = proj.shape
    nj = heads // hb
    assert (t // GDN_CHUNK) % (2 * cg) == 0 and heads % hb == 0
    wblk = hb * LANE
    kern = functools.partial(_gdn_kernel, hb=hb, cg=cg, scale=float(LANE) ** -0.5)
    section = lambda n: pl.BlockSpec((None, 1, t, wblk), lambda i, j: (n, i, 0, j))
    return pl.pallas_call(
        kern,
        grid=(b, nj),
        in_specs=[section(0), section(1), section(2), section(3),
                  pl.BlockSpec((1, t, LANE), lambda i, j: (i, 0, 0)),
                  pl.BlockSpec((1, t, LANE), lambda i, j: (i, 0, 0)),
                  pl.BlockSpec((conv_w.shape[0], wblk), lambda i, j: (0, j)),
                  pl.BlockSpec((conv_w.shape[0], wblk), lambda i, j: (0, nj + j)),
                  pl.BlockSpec((conv_w.shape[0], wblk), lambda i, j: (0, 2 * nj + j)),
                  pl.BlockSpec((1, LANE), lambda i, j: (0, 0))],
        out_specs=pl.BlockSpec((1, t, wblk), lambda i, j: (i, 0, j)),
        out_shape=jax.ShapeDtypeStruct((b, t, heads * LANE), BF16),
        scratch_shapes=[pltpu.VMEM((3 * hb, t + CONV_HALO, LANE), F32),
                        pltpu.VMEM((2, hb, cg, GDN_CHUNK, LANE), F32),
                        pltpu.VMEM((2, hb, cg, 2 * GDN_CHUNK, LANE), BF16),
                        pltpu.VMEM((2, hb, cg, 2 * GDN_CHUNK, LANE), BF16),
                        pltpu.VMEM((2, hb, cg, SUBLANE, LANE), F32)],
        compiler_params=pltpu.CompilerParams(
            dimension_semantics=("parallel", "arbitrary"), vmem_limit_bytes=VMEM_LIMIT),
        name="gdn_delta_rule",
    )(proj, proj, proj, proj, beta, gc, conv_w, conv_w, conv_w, norm_w)


def _pool_kernel(*refs, rows, ncast):
    p_ref, w_ref, sc_ref = refs[:3]
    o_ref, pad_ref = refs[3 + ncast], refs[-1]
    _do_casts(refs[3:3 + ncast], refs[4 + ncast:-1])
    t = p_ref.shape[1]
    cgd = w_ref.shape[1]
    halo = POOL_HALO
    assert halo >= max(POOL_WINDOWS) - 1
    nslab = cgd // LANE
    for g, win in enumerate(POOL_WINDOWS):
        sl = slice(g * cgd, (g + 1) * cgd)
        for k in range(nslab):
            pad_ref[k, 0:halo, :] = jnp.zeros((halo, LANE), F32)
            pad_ref[k, halo:halo + t, :] = p_ref[0, :, g * cgd + k * LANE:g * cgd + (k + 1) * LANE]

        def body(i, carry, win=win, sl=sl, g=g):
            r0 = pl.multiple_of(i * rows, rows)

            def shifted(d):
                return jnp.concatenate(
                    [pad_ref[k, pl.ds(r0 + halo - d, rows), :] for k in range(nslab)], axis=1)

            x = shifted(0)
            ssum = x
            for d in range(1, win):
                ssum = ssum + shifted(d)
            pos = r0 + lax.broadcasted_iota(jnp.int32, (rows, cgd), 0)
            cnt = jnp.minimum(pos + 1, win).astype(F32)
            pooled = ssum / cnt - x
            mixed = _dot(pooled.astype(BF16), w_ref[g]) * sc_ref[:, sl]
            o_ref[0, pl.ds(r0, rows), sl] = mixed.astype(o_ref.dtype)
            return carry

        lax.fori_loop(0, t // rows, body, 0)


def _pool(proj, pool_w16, pool_scale, p_block, casts):
    b, t, _ = proj.shape
    g, cgd, _ = pool_w16.shape
    width = g * cgd
    cin, cout, cshape = _cast_specs(casts, b, lambda i: i)
    return pl.pallas_call(
        functools.partial(_pool_kernel, rows=POOL_ROWS, ncast=len(casts)),
        grid=(b,),
        in_specs=[pl.BlockSpec((1, t, width), lambda i: (i, 0, p_block)),
                  pl.BlockSpec((g, cgd, cgd), lambda i: (0, 0, 0)),
                  pl.BlockSpec((1, width), lambda i: (0, 0))] + cin,
        out_specs=[pl.BlockSpec((1, t, width), lambda i: (i, 0, 0))] + cout,
        out_shape=[jax.ShapeDtypeStruct((b, t, width), BF16)] + cshape,
        scratch_shapes=[pltpu.VMEM((cgd // LANE, t + POOL_HALO, LANE), F32)],
        compiler_params=pltpu.CompilerParams(
            dimension_semantics=("parallel",), vmem_limit_bytes=VMEM_LIMIT),
        name="multiscale_pool",
    )(proj, pool_w16, pool_scale, *casts)


def _outproj_kernel(*refs, alpha, rb, ncast):
    og_ref, op_ref, x_ref, wa_ref, wb_ref, g_ref, b_ref = refs[:7]
    o_ref = refs[7 + ncast]
    _do_casts(refs[7:7 + ncast], refs[8 + ncast:])
    for r in range(o_ref.shape[0] // rb):
        rows = slice(r * rb, (r + 1) * rb)
        mix = _dot(og_ref[rows, :], wa_ref[...]) + _dot(op_ref[rows, :], wb_ref[...])
        o_ref[rows, :] = _layer_norm(alpha * x_ref[rows, :] + mix, g_ref[...], b_ref[...])


def _outproj(og, op, x, w, g, b, alpha, tm, rb, casts):
    m, d = x.shape
    ka, kb = og.shape[1], op.shape[1]
    assert ka % kb == 0 and w.shape[0] == ka + kb
    const = lambda i: (0, 0)
    once = pl.Buffered(1)
    cin, cout, cshape = _cast_specs(casts, m // tm, lambda i: i)
    return pl.pallas_call(
        functools.partial(_outproj_kernel, alpha=alpha, rb=rb, ncast=len(casts)),
        grid=(m // tm,),
        in_specs=[pl.BlockSpec((tm, ka), lambda i: (i, 0)),
                  pl.BlockSpec((tm, kb), lambda i: (i, 0)),
                  pl.BlockSpec((tm, d), lambda i: (i, 0)),
                  pl.BlockSpec((ka, d), const, pipeline_mode=once),
                  pl.BlockSpec((kb, d), lambda i: (ka // kb, 0), pipeline_mode=once),
                  pl.BlockSpec((1, d), const),
                  pl.BlockSpec((1, d), const)] + cin,
        out_specs=[pl.BlockSpec((tm, d), lambda i: (i, 0))] + cout,
        out_shape=[jax.ShapeDtypeStruct((m, d), F32)] + cshape,
        compiler_params=pltpu.CompilerParams(
            dimension_semantics=("parallel",), vmem_limit_bytes=VMEM_LIMIT),
        name="outproj_ln",
    )(og, op, x, w, w, g, b, *casts)


def _xattn_kernel(*refs, alpha, heads, rb, ncast):
    h_ref, wq_ref, k_ref, v_ref, wo_ref, g_ref, b_ref = refs[:7]
    o_ref = refs[7 + ncast]
    _do_casts(refs[7:7 + ncast], refs[8 + ncast:])
    d = h_ref.shape[-1]
    hd = d // heads
    for r in range(h_ref.shape[1] // rb):
        rows = slice(r * rb, (r + 1) * rb)
        h = h_ref[0, rows, :]
        q = _dot(h.astype(BF16), wq_ref[...])
        outs = []
        for i in range(heads):
            qh = q[:, i * hd:(i + 1) * hd].astype(BF16)
            kh = k_ref[0, :, i * hd:(i + 1) * hd]
            vh = v_ref[0, :, i * hd:(i + 1) * hd]
            s = lax.dot_general(qh, kh, _NT, preferred_element_type=F32) * (float(hd) ** -0.5)
            e = jnp.exp(s - jnp.max(s, axis=-1, keepdims=True))
            p = e / jnp.sum(e, axis=-1, keepdims=True)
            outs.append(_dot(p.astype(BF16), vh).astype(BF16))
        xa = _dot(jnp.concatenate(outs, axis=1), wo_ref[...])
        o_ref[0, rows, :] = _layer_norm(alpha * h + xa, g_ref[...], b_ref[...])


def _xattn(h, wq, k, v, wo, g, b, alpha, tm, rb, casts):
    bsz, t, d = h.shape
    mlen = k.shape[1]
    nj = t // tm
    const = lambda i, j: (0, 0)
    once = pl.Buffered(1)
    cin, cout, cshape = _cast_specs(casts, bsz * nj, lambda i, j: i * nj + j)
    return pl.pallas_call(
        functools.partial(_xattn_kernel, alpha=alpha, heads=XATTN_HEADS, rb=rb, ncast=len(casts)),
        grid=(bsz, nj),
        in_specs=[pl.BlockSpec((1, tm, d), lambda i, j: (i, j, 0)),
                  pl.BlockSpec((d, d), const, pipeline_mode=once),
                  pl.BlockSpec((1, mlen, d), lambda i, j: (i, 0, 0)),
                  pl.BlockSpec((1, mlen, d), lambda i, j: (i, 0, 0)),
                  pl.BlockSpec((d, d), const, pipeline_mode=once),
                  pl.BlockSpec((1, d), const),
                  pl.BlockSpec((1, d), const)] + cin,
        out_specs=[pl.BlockSpec((1, tm, d), lambda i, j: (i, j, 0))] + cout,
        out_shape=[jax.ShapeDtypeStruct((bsz, t, d), F32)] + cshape,
        compiler_params=pltpu.CompilerParams(
            dimension_semantics=("parallel", "parallel"), vmem_limit_bytes=VMEM_LIMIT),
        name="xattn_ln",
    )(h, wq, k, v, wo, g, b, *casts)


def _mlp_kernel(h_ref, wu_ref, wd_ref, g_ref, b_ref, o_ref, *, alpha, rb):
    j = pl.program_id(1)

    @pl.when(j == 0)
    def _():
        o_ref[...] = jnp.zeros_like(o_ref)

    def ff(rows):
        u = jnp.maximum(_dot(h_ref[rows, :].astype(BF16), wu_ref[...]), 0.0)
        return _dot((u * u).astype(BF16), wd_ref[...])

    last = pl.num_programs(1) - 1

    @pl.when(j < last)
    def _():
        o_ref[...] += ff(slice(None))

    @pl.when(j == last)
    def _():
        for r in range(o_ref.shape[0] // rb):
            rows = slice(r * rb, (r + 1) * rb)
            y = alpha * h_ref[rows, :] + (o_ref[rows, :] + ff(rows))
            o_ref[rows, :] = _layer_norm(y, g_ref[...], b_ref[...])


def _mlp(h, wu, wd, g, b, alpha, tm, tf, rb):
    m, d = h.shape
    f = wu.shape[1]
    const = lambda i, j: (0, 0)
    return pl.pallas_call(
        functools.partial(_mlp_kernel, alpha=alpha, rb=rb),
        grid=(m // tm, f // tf),
        in_specs=[pl.BlockSpec((tm, d), lambda i, j: (i, 0)),
                  pl.BlockSpec((d, tf), lambda i, j: (0, j)),
                  pl.BlockSpec((tf, d), lambda i, j: (j, 0)),
                  pl.BlockSpec((1, d), const),
                  pl.BlockSpec((1, d), const)],
        out_specs=pl.BlockSpec((tm, d), lambda i, j: (i, 0)),
        out_shape=jax.ShapeDtypeStruct((m, d), F32),
        compiler_params=pltpu.CompilerParams(
            dimension_semantics=("parallel", "arbitrary"), vmem_limit_bytes=VMEM_LIMIT),
        name="mlp_ln",
    )(h, wu, wd, g, b)


def _pad_lanes(v):
    return jnp.pad(v.astype(F32), (0, LANE - v.shape[0]))[None, :]


def kernel(x, mem, w_in, conv_w, a_log, dt_bias, gdn_norm_w, pool_w, pool_scale, w_out, ln1_g, ln1_b,
           xq_w, xk_w, xv_w, xo_w, ln2_g, ln2_b, w_up, w_down, ln3_g, ln3_b):
    bsz, t, d = x.shape
    depth = w_in.shape[0]
    heads = a_log.shape[1]
    gw = conv_w.shape[2] // 3
    pw = pool_w.shape[1] * pool_w.shape[2]
    assert gw == heads * LANE and t % GDN_CHUNK == 0 and heads <= LANE
    alpha = (2.0 * depth) ** 0.25
    row2 = lambda v: v[None, :]

    h = x
    for l in range(depth):
        w16 = w_in[l].astype(BF16)
        zpad = jnp.zeros((d, LANE - heads), BF16)
        w_tail = jnp.concatenate(
            [w16[:, 4 * gw + 2 * heads:], w16[:, 4 * gw:4 * gw + heads], zpad,
             w16[:, 4 * gw + heads:4 * gw + 2 * heads], zpad], axis=1)
        assert pw % (2 * LANE) == 0

        proj, tail = _inproj(h.reshape(bsz * t, d), w16, 4 * gw, w_tail, tm=INPROJ_TM, tn=gw, sectioned=True)
        proj = proj.reshape(4, bsz, t, gw)
        tail = tail.reshape(bsz, t, pw + 2 * LANE)
        beta, gc = _gates(tail, _pad_lanes(a_log[l]), _pad_lanes(dt_bias[l]), pw // (2 * LANE))
        o_gdn = _gdn(proj, beta, gc, conv_w[l], row2(gdn_norm_w[l]), heads,
                     hb=GDN_HEADS_PER_STEP, cg=GDN_CHUNKS_PER_GROUP)
        o_pool, wo16, xk16, xv16 = _pool(tail, pool_w[l].astype(BF16), row2(pool_scale[l]), 0,
                                         casts=(w_out[l], xk_w[l], xv_w[l]))

        h1, xq16, xo16 = _outproj(o_gdn.reshape(bsz * t, gw), o_pool.reshape(bsz * t, pw),
                                  h.reshape(bsz * t, d), wo16, row2(ln1_g[l]), row2(ln1_b[l]), alpha,
                                  tm=OUTPROJ_TM, rb=OUTPROJ_RB, casts=(xq_w[l], xo_w[l]))

        mlen = mem.shape[1]
        mk, mv = _inproj(mem.reshape(bsz * mlen, d), xk16, d, xv16, tm=INPROJ_TM, tn=KVPROJ_TN, out_dtype=BF16)
        h2, wu16, wd16 = _xattn(h1.reshape(bsz, t, d), xq16, mk.reshape(bsz, mlen, d), mv.reshape(bsz, mlen, d),
                                xo16, row2(ln2_g[l]), row2(ln2_b[l]), alpha, tm=XATTN_TM, rb=XATTN_RB,
                                casts=(w_up[l], w_down[l]))

        h3 = _mlp(h2.reshape(bsz * t, d), wu16, wd16,
                  row2(ln3_g[l]), row2(ln3_b[l]), alpha, tm=MLP_TM, tf=MLP_TF, rb=MLP_RB)
        h = h3.reshape(bsz, t, d)
    return h
```

```python
import functools

import jax
import jax.numpy as jnp
from jax import lax
from jax.experimental import pallas as pl
from jax.experimental.pallas import tpu as pltpu

F32 = jnp.float32
BF16 = jnp.bfloat16

LANE = 128
GDN_CHUNK = 128
POOL_WINDOWS = (2, 4, 8, 16)
XATTN_HEADS = 4
LN_EPS = 1e-5
NORM_EPS = 1e-6
LOG2E = 1.4426950408889634
SUBLANE = 8
VMEM_LIMIT = 60 * 1024 * 1024
CONV_HALO = SUBLANE
POOL_HALO = 2 * SUBLANE

INPROJ_TM = 1024
KVPROJ_TN = 1024
OUTPROJ_TM, OUTPROJ_RB = 1024, 256
XATTN_TM, XATTN_RB = 512, 256
MLP_TM, MLP_TF, MLP_RB = 1024, 1024, 256
GDN_HEADS_PER_STEP = 2
GDN_CHUNKS_PER_GROUP = 4
POOL_ROWS = 256

_NT = (((1,), (1,)), ((), ()))


def _dot(a, b):
    return jnp.dot(a, b, preferred_element_type=F32)


def _layer_norm(y, g, b):
    mu = jnp.mean(y, axis=-1, keepdims=True)
    yc = y - mu
    var = jnp.mean(yc * yc, axis=-1, keepdims=True)
    return yc * lax.rsqrt(var + LN_EPS) * g + b


def _silu(x):
    return x * jax.nn.sigmoid(x)


def _cast_specs(mats, nsteps, step_of):
    ins, outs, shapes = [], [], []
    for w in mats:
        r, c = w.shape
        assert r % (16 * nsteps) == 0
        spec = pl.BlockSpec((r // nsteps, c), lambda *ids: (step_of(*ids), 0))
        ins.append(spec)
        outs.append(spec)
        shapes.append(jax.ShapeDtypeStruct((r, c), BF16))
    return ins, outs, shapes


def _do_casts(src_refs, dst_refs):
    for s, d in zip(src_refs, dst_refs):
        d[...] = s[...].astype(d.dtype)


def _inproj_kernel(x_ref, wm_ref, wt_ref, om_ref, ot_ref, xb_ref, *, nmain):
    j = pl.program_id(1)

    @pl.when(j == 0)
    def _():
        xb_ref[...] = x_ref[...].astype(BF16)

    @pl.when(j < nmain)
    def _():
        om_ref[...] = _dot(xb_ref[...], wm_ref[...]).astype(om_ref.dtype)

    @pl.when(j == nmain)
    def _():
        ot_ref[...] = _dot(xb_ref[...], wt_ref[...]).astype(ot_ref.dtype)


def _inproj(x, w_all, n_main, w_tail, tm, tn, out_dtype=F32, sectioned=False):
    m, k = x.shape
    nt = w_tail.shape[1]
    nmain = n_main // tn
    main_idx = lambda j: jnp.minimum(j, nmain - 1)
    if sectioned:
        main_spec = pl.BlockSpec((None, tm, tn), lambda i, j: (main_idx(j), i, 0))
        main_shape = jax.ShapeDtypeStruct((nmain, m, tn), out_dtype)
    else:
        main_spec = pl.BlockSpec((tm, tn), lambda i, j: (i, main_idx(j)))
        main_shape = jax.ShapeDtypeStruct((m, n_main), out_dtype)
    return pl.pallas_call(
        functools.partial(_inproj_kernel, nmain=nmain),
        grid=(m // tm, nmain + 1),
        in_specs=[pl.BlockSpec((tm, k), lambda i, j: (i, 0)),
                  pl.BlockSpec((k, tn), lambda i, j: (0, main_idx(j))),
                  pl.BlockSpec((k, nt), lambda i, j: (0, 0), pipeline_mode=pl.Buffered(1))],
        out_specs=[main_spec, pl.BlockSpec((tm, nt), lambda i, j: (i, 0))],
        out_shape=[main_shape, jax.ShapeDtypeStruct((m, nt), out_dtype)],
        scratch_shapes=[pltpu.VMEM((tm, k), BF16)],
        compiler_params=pltpu.CompilerParams(
            dimension_semantics=("parallel", "arbitrary"), vmem_limit_bytes=VMEM_LIMIT),
        name="inproj",
    )(x, w_all, w_tail)


def _gates_kernel(ba_ref, alog_ref, dtb_ref, beta_ref, gc_ref):
    t = ba_ref.shape[1]
    c = GDN_CHUNK
    beta_ref[0] = jax.nn.sigmoid(ba_ref[0, :, :LANE])
    g = -jnp.exp(alog_ref[...]) * jax.nn.softplus(ba_ref[0, :, LANE:] + dtb_ref[...])
    row = lax.broadcasted_iota(jnp.int32, (c, c), 0)
    col = lax.broadcasted_iota(jnp.int32, (c, c), 1)
    tri = (row >= col).astype(F32)
    for i in range(t // c):
        gc_ref[0, i * c:(i + 1) * c, :] = LOG2E * jnp.dot(
            tri, g[i * c:(i + 1) * c], precision=lax.Precision.HIGHEST, preferred_element_type=F32)


def _gates(proj, alog_pad, dtb_pad, ba_block):
    b, t, _ = proj.shape
    return pl.pallas_call(
        _gates_kernel,
        grid=(b,),
        in_specs=[pl.BlockSpec((1, t, 2 * LANE), lambda i: (i, 0, ba_block)),
                  pl.BlockSpec((1, LANE), lambda i: (0, 0)),
                  pl.BlockSpec((1, LANE), lambda i: (0, 0))],
        out_specs=[pl.BlockSpec((1, t, LANE), lambda i: (i, 0, 0)),
                   pl.BlockSpec((1, t, LANE), lambda i: (i, 0, 0))],
        out_shape=[jax.ShapeDtypeStruct((b, t, LANE), F32)] * 2,
        compiler_params=pltpu.CompilerParams(dimension_semantics=("parallel",)),
        name="gdn_gates",
    )(proj, alog_pad, dtb_pad)


def _gdn_kernel(q_ref, k_ref, v_ref, z_ref, beta_ref, gc_ref, cwq_ref, cwk_ref, cwv_ref, nw_ref,
                o_ref, pad_ref, u_s, wq_s, ak_s, dl_s, *, hb, cg, scale):
    t = q_ref.shape[1]
    c = GDN_CHUNK
    nc = t // c
    conv_k = cwq_ref.shape[0]
    halo = CONV_HALO
    assert halo >= conv_k - 1
    jh = pl.program_id(1)

    for slot, src in enumerate((q_ref, k_ref, v_ref)):
        for s in range(hb):
            pad_ref[slot * hb + s, 0:halo, :] = jnp.zeros((halo, LANE), F32)
            pad_ref[slot * hb + s, halo:halo + t, :] = src[0, :, s * LANE:(s + 1) * LANE]

    row = lax.broadcasted_iota(jnp.int32, (c, c), 0)
    col = lax.broadcasted_iota(jnp.int32, (c, c), 1)
    incl = row >= col
    strict = row > col
    eye = (row == col).astype(F32)
    lane = lax.broadcasted_iota(jnp.int32, (c, LANE), 1)

    n_sq = (c - 1).bit_length() - 1
    n_prep = hb * cg * (4 + n_sq)
    n_seq = cg * (2 + hb)

    def prep_group(g, sb):
        st = []
        for s, cc in [(s, cc) for s in range(hb) for cc in range(cg)]:
            ci = g * cg + cc
            r0 = pl.multiple_of(ci * c, c)
            sl = slice(s * LANE, (s + 1) * LANE)

            def conv_act(slot, cw_ref):
                slab = slot * hb + s
                acc = cw_ref[0:1, sl] * pad_ref[slab, pl.ds(r0 + halo - conv_k + 1, c), :]
                for j in range(1, conv_k):
                    acc = acc + cw_ref[j:j + 1, sl] * pad_ref[slab, pl.ds(r0 + halo - conv_k + 1 + j, c), :]
                return _silu(acc)

            q = conv_act(0, cwq_ref)
            k = conv_act(1, cwk_ref)
            v = conv_act(2, cwv_ref)
            q = q * lax.rsqrt(jnp.sum(q * q, axis=-1, keepdims=True) + NORM_EPS) * scale
            k = k * lax.rsqrt(jnp.sum(k * k, axis=-1, keepdims=True) + NORM_EPS)

            hm = lane == (jh * hb + s)
            bcol = jnp.sum(jnp.where(hm, beta_ref[0, pl.ds(r0, c), :], 0.0), axis=1, keepdims=True)
            gcol = jnp.sum(jnp.where(hm, gc_ref[0, pl.ds(r0, c), :], 0.0), axis=1, keepdims=True)
            gcb = jnp.broadcast_to(gcol, (c, LANE))
            gct = gcb.T
            e = jnp.exp2(jnp.where(incl, gcb - gct, 0.0))
            ecol = jnp.exp2(gcol)
            kb = k * bcol
            glast = gcb[c - 1:c, :]
            dl_s[sb, s, cc] = jnp.broadcast_to(jnp.exp2(glast), (SUBLANE, LANE))
            st.append(dict(
                s=s, cc=cc, dec=jnp.where(incl, e, 0.0), ndec=jnp.where(strict, -e, 0.0),
                kbq=jnp.concatenate([kb, q], axis=0).astype(BF16), k16=k.astype(BF16),
                rhs=jnp.concatenate([v * bcol, kb * ecol], axis=1).astype(BF16),
                qg=(q * ecol).astype(BF16),
                kdt=(k * jnp.exp2(glast - gcb)).T.astype(BF16)))
            yield

        for d in st:
            kq = lax.dot_general(d["kbq"], d["k16"], _NT, preferred_element_type=F32)
            d["attn"] = (kq[c:] * d["dec"]).astype(BF16)
            d["x"] = kq[:c] * d["ndec"]
            d["p"] = eye + d["x"]
            yield
        for d in st:
            x16 = d["x"].astype(BF16)
            d["x"] = _dot(x16, x16)
            yield
        for n in range(1, n_sq + 1):
            for d in st:
                x16 = d["x"].astype(BF16)
                if n < n_sq:
                    r = _dot(jnp.concatenate([x16, d["p"].astype(BF16)], axis=0), x16)
                    d["x"] = r[:c]
                    d["p"] = d["p"] + r[c:]
                else:
                    d["p"] = d["p"] + _dot(d["p"].astype(BF16), x16)
                yield
        for d in st:
            s, cc = d["s"], d["cc"]
            uw = _dot(d["p"].astype(BF16), d["rhs"])
            u_s[sb, s, cc] = uw[:, :LANE]
            wq_s[sb, s, cc] = jnp.concatenate([uw[:, LANE:].astype(BF16), d["qg"]], axis=0)
            ak_s[sb, s, cc] = jnp.concatenate([d["attn"], d["kdt"]], axis=0)
            yield

    def seq_group(g, sb, states):
        for cc in range(cg):
            r0 = pl.multiple_of((g * cg + cc) * c, c)
            rs = [_dot(wq_s[sb, s, cc], states[s].astype(BF16)) for s in range(hb)]
            yield
            vns = [(u_s[sb, s, cc] - rs[s][:c]).astype(BF16) for s in range(hb)]
            r2s = [_dot(ak_s[sb, s, cc], vns[s]) for s in range(hb)]
            yield
            for s in range(hb):
                sl = slice(s * LANE, (s + 1) * LANE)
                states[s] = states[s] * dl_s[sb, s, cc][0:1, :] + r2s[s][c:]
                o = rs[s][c:] + r2s[s][:c]
                z = z_ref[0, pl.ds(r0, c), sl]
                o = o * lax.rsqrt(jnp.mean(o * o, axis=-1, keepdims=True) + NORM_EPS) * nw_ref[...]
                o_ref[0, pl.ds(r0, c), sl] = (o * _silu(z)).astype(o_ref.dtype)
                yield

    def weave(seq, prep):
        done = 0
        for i in range(n_prep):
            while done * n_prep <= i * n_seq and done < n_seq:
                next(seq)
                done += 1
            next(prep)
        for _ in range(done, n_seq):
            next(seq)

    u_s[1] = jnp.zeros(u_s.shape[1:], u_s.dtype)
    wq_s[1] = jnp.zeros(wq_s.shape[1:], wq_s.dtype)
    ak_s[1] = jnp.zeros(ak_s.shape[1:], ak_s.dtype)
    dl_s[1] = jnp.zeros(dl_s.shape[1:], dl_s.dtype)
    ng = nc // cg

    def pipe_body(m, states):
        states = list(states)
        weave(seq_group(jnp.maximum(2 * m - 1, 0), 1, states), prep_group(2 * m, 0))
        weave(seq_group(2 * m, 0, states), prep_group(2 * m + 1, 1))
        return tuple(states)

    states = lax.fori_loop(0, ng // 2, pipe_body, tuple(jnp.zeros((LANE, LANE), F32) for _ in range(hb)))
    for _ in seq_group(ng - 1, 1, list(states)):
        pass


def _gdn(proj, beta, gc, conv_w, norm_w, heads, hb, cg):
    _, b, t, _ = proj.shape
    nj = heads // hb
    assert (t // GDN_CHUNK) % (2 * cg) == 0 and heads % hb == 0
    wblk = hb * LANE
    kern = functools.partial(_gdn_kernel, hb=hb, cg=cg, scale=float(LANE) ** -0.5)
    section = lambda n: pl.BlockSpec((None, 1, t, wblk), lambda i, j: (n, i, 0, j))
    return pl.pallas_call(
        kern,
        grid=(b, nj),
        in_specs=[section(0), section(1), section(2), section(3),
                  pl.BlockSpec((1, t, LANE), lambda i, j: (i, 0, 0)),
                  pl.BlockSpec((1, t, LANE), lambda i, j: (i, 0, 0)),
                  pl.BlockSpec((conv_w.shape[0], wblk), lambda i, j: (0, j)),
                  pl.BlockSpec((conv_w.shape[0], wblk), lambda i, j: (0, nj + j)),
                  pl.BlockSpec((conv_w.shape[0], wblk), lambda i, j: (0, 2 * nj + j)),
                  pl.BlockSpec((1, LANE), lambda i, j: (0, 0))],
        out_specs=pl.BlockSpec((1, t, wblk), lambda i, j: (i, 0, j)),
        out_shape=jax.ShapeDtypeStruct((b, t, heads * LANE), BF16),
        scratch_shapes=[pltpu.VMEM((3 * hb, t + CONV_HALO, LANE), F32),
                        pltpu.VMEM((2, hb, cg, GDN_CHUNK, LANE), F32),
                        pltpu.VMEM((2, hb, cg, 2 * GDN_CHUNK, LANE), BF16),
                        pltpu.VMEM((2, hb, cg, 2 * GDN_CHUNK, LANE), BF16),
                        pltpu.VMEM((2, hb, cg, SUBLANE, LANE), F32)],
        compiler_params=pltpu.CompilerParams(
            dimension_semantics=("parallel", "arbitrary"), vmem_limit_bytes=VMEM_LIMIT),
        name="gdn_delta_rule",
    )(proj, proj, proj, proj, beta, gc, conv_w, conv_w, conv_w, norm_w)


def _pool_kernel(*refs, rows, ncast):
    p_ref, w_ref, sc_ref = refs[:3]
    o_ref, pad_ref = refs[3 + ncast], refs[-1]
    _do_casts(refs[3:3 + ncast], refs[4 + ncast:-1])
    t = p_ref.shape[1]
    cgd = w_ref.shape[1]
    halo = POOL_HALO
    assert halo >= max(POOL_WINDOWS) - 1
    nslab = cgd // LANE
    for g, win in enumerate(POOL_WINDOWS):
        sl = slice(g * cgd, (g + 1) * cgd)
        for k in range(nslab):
            pad_ref[k, 0:halo, :] = jnp.zeros((halo, LANE), F32)
            pad_ref[k, halo:halo + t, :] = p_ref[0, :, g * cgd + k * LANE:g * cgd + (k + 1) * LANE]

        def body(i, carry, win=win, sl=sl, g=g):
            r0 = pl.multiple_of(i * rows, rows)

            def shifted(d):
                return jnp.concatenate(
                    [pad_ref[k, pl.ds(r0 + halo - d, rows), :] for k in range(nslab)], axis=1)

            x = shifted(0)
            ssum = x
            for d in range(1, win):
                ssum = ssum + shifted(d)
            pos = r0 + lax.broadcasted_iota(jnp.int32, (rows, cgd), 0)
            cnt = jnp.minimum(pos + 1, win).astype(F32)
            pooled = ssum / cnt - x
            mixed = _dot(pooled.astype(BF16), w_ref[g]) * sc_ref[:, sl]
            o_ref[0, pl.ds(r0, rows), sl] = mixed.astype(o_ref.dtype)
            return carry

        lax.fori_loop(0, t // rows, body, 0)


def _pool(proj, pool_w16, pool_scale, p_block, casts):
    b, t, _ = proj.shape
    g, cgd, _ = pool_w16.shape
    width = g * cgd
    cin, cout, cshape = _cast_specs(casts, b, lambda i: i)
    return pl.pallas_call(
        functools.partial(_pool_kernel, rows=POOL_ROWS, ncast=len(casts)),
        grid=(b,),
        in_specs=[pl.BlockSpec((1, t, width), lambda i: (i, 0, p_block)),
                  pl.BlockSpec((g, cgd, cgd), lambda i: (0, 0, 0)),
                  pl.BlockSpec((1, width), lambda i: (0, 0))] + cin,
        out_specs=[pl.BlockSpec((1, t, width), lambda i: (i, 0, 0))] + cout,
        out_shape=[jax.ShapeDtypeStruct((b, t, width), BF16)] + cshape,
        scratch_shapes=[pltpu.VMEM((cgd // LANE, t + POOL_HALO, LANE), F32)],
        compiler_params=pltpu.CompilerParams(
            dimension_semantics=("parallel",), vmem_limit_bytes=VMEM_LIMIT),
        name="multiscale_pool",
    )(proj, pool_w16, pool_scale, *casts)


def _outproj_kernel(*refs, alpha, rb, ncast):
    og_ref, op_ref, x_ref, wa_ref, wb_ref, g_ref, b_ref = refs[:7]
    o_ref = refs[7 + ncast]
    _do_casts(refs[7:7 + ncast], refs[8 + ncast:])
    for r in range(o_ref.shape[0] // rb):
        rows = slice(r * rb, (r + 1) * rb)
        mix = _dot(og_ref[rows, :], wa_ref[...]) + _dot(op_ref[rows, :], wb_ref[...])
        o_ref[rows, :] = _layer_norm(alpha * x_ref[rows, :] + mix, g_ref[...], b_ref[...])


def _outproj(og, op, x, w, g, b, alpha, tm, rb, casts):
    m, d = x.shape
    ka, kb = og.shape[1], op.shape[1]
    assert ka % kb == 0 and w.shape[0] == ka + kb
    const = lambda i: (0, 0)
    once = pl.Buffered(1)
    cin, cout, cshape = _cast_specs(casts, m // tm, lambda i: i)
    return pl.pallas_call(
        functools.partial(_outproj_kernel, alpha=alpha, rb=rb, ncast=len(casts)),
        grid=(m // tm,),
        in_specs=[pl.BlockSpec((tm, ka), lambda i: (i, 0)),
                  pl.BlockSpec((tm, kb), lambda i: (i, 0)),
                  pl.BlockSpec((tm, d), lambda i: (i, 0)),
                  pl.BlockSpec((ka, d), const, pipeline_mode=once),
                  pl.BlockSpec((kb, d), lambda i: (ka // kb, 0), pipeline_mode=once),
                  pl.BlockSpec((1, d), const),
                  pl.BlockSpec((1, d), const)] + cin,
        out_specs=[pl.BlockSpec((tm, d), lambda i: (i, 0))] + cout,
        out_shape=[jax.ShapeDtypeStruct((m, d), F32)] + cshape,
        compiler_params=pltpu.CompilerParams(
            dimension_semantics=("parallel",), vmem_limit_bytes=VMEM_LIMIT),
        name="outproj_ln",
    )(og, op, x, w, w, g, b, *casts)


def _xattn_kernel(*refs, alpha, heads, rb, ncast):
    h_ref, wq_ref, k_ref, v_ref, wo_ref, g_ref, b_ref = refs[:7]
    o_ref = refs[7 + ncast]
    _do_casts(refs[7:7 + ncast], refs[8 + ncast:])
    d = h_ref.shape[-1]
    hd = d // heads
    for r in range(h_ref.shape[1] // rb):
        rows = slice(r * rb, (r + 1) * rb)
        h = h_ref[0, rows, :]
        q = _dot(h.astype(BF16), wq_ref[...])
        outs = []
        for i in range(heads):
            qh = q[:, i * hd:(i + 1) * hd].astype(BF16)
            kh = k_ref[0, :, i * hd:(i + 1) * hd]
            vh = v_ref[0, :, i * hd:(i + 1) * hd]
            s = lax.dot_general(qh, kh, _NT, preferred_element_type=F32) * (float(hd) ** -0.5)
            e = jnp.exp(s - jnp.max(s, axis=-1, keepdims=True))
            p = e / jnp.sum(e, axis=-1, keepdims=True)
            outs.append(_dot(p.astype(BF16), vh).astype(BF16))
        xa = _dot(jnp.concatenate(outs, axis=1), wo_ref[...])
        o_ref[0, rows, :] = _layer_norm(alpha * h + xa, g_ref[...], b_ref[...])


def _xattn(h, wq, k, v, wo, g, b, alpha, tm, rb, casts):
    bsz, t, d = h.shape
    mlen = k.shape[1]
    nj = t // tm
    const = lambda i, j: (0, 0)
    once = pl.Buffered(1)
    cin, cout, cshape = _cast_specs(casts, bsz * nj, lambda i, j: i * nj + j)
    return pl.pallas_call(
        functools.partial(_xattn_kernel, alpha=alpha, heads=XATTN_HEADS, rb=rb, ncast=len(casts)),
        grid=(bsz, nj),
        in_specs=[pl.BlockSpec((1, tm, d), lambda i, j: (i, j, 0)),
                  pl.BlockSpec((d, d), const, pipeline_mode=once),
                  pl.BlockSpec((1, mlen, d), lambda i, j: (i, 0, 0)),
                  pl.BlockSpec((1, mlen, d), lambda i, j: (i, 0, 0)),
                  pl.BlockSpec((d, d), const, pipeline_mode=once),
                  pl.BlockSpec((1, d), const),
                  pl.BlockSpec((1, d), const)] + cin,
        out_specs=[pl.BlockSpec((1, tm, d), lambda i, j: (i, j, 0))] + cout,
        out_shape=[jax.ShapeDtypeStruct((bsz, t, d), F32)] + cshape,
        compiler_params=pltpu.CompilerParams(
            dimension_semantics=("parallel", "parallel"), vmem_limit_bytes=VMEM_LIMIT),
        name="xattn_ln",
    )(h, wq, k, v, wo, g, b, *casts)


def _mlp_kernel(h_ref, wu_ref, wd_ref, g_ref, b_ref, o_ref, *, alpha, rb):
    j = pl.program_id(1)
    last = pl.num_programs(1) - 1

    def ff(rows):
        u = jnp.maximum(_dot(h_ref[rows, :].astype(BF16), wu_ref[...]), 0.0)
        return _dot((u * u).astype(BF16), wd_ref[...])

    @pl.when(j == 0)
    def _():
        o_ref[...] = ff(slice(None))

    @pl.when(jnp.logical_and(j > 0, j < last))
    def _():
        o_ref[...] += ff(slice(None))

    @pl.when(j == last)
    def _():
        for r in range(o_ref.shape[0] // rb):
            rows = slice(r * rb, (r + 1) * rb)
            y = alpha * h_ref[rows, :] + (o_ref[rows, :] + ff(rows))
            o_ref[rows, :] = _layer_norm(y, g_ref[...], b_ref[...])


def _mlp(h, wu, wd, g, b, alpha, tm, tf, rb):
    m, d = h.shape
    f = wu.shape[1]
    assert f // tf >= 2
    const = lambda i, j: (0, 0)
    return pl.pallas_call(
        functools.partial(_mlp_kernel, alpha=alpha, rb=rb),
        grid=(m // tm, f // tf),
        in_specs=[pl.BlockSpec((tm, d), lambda i, j: (i, 0)),
                  pl.BlockSpec((d, tf), lambda i, j: (0, j)),
                  pl.BlockSpec((tf, d), lambda i, j: (j, 0)),
                  pl.BlockSpec((1, d), const),
                  pl.BlockSpec((1, d), const)],
        out_specs=pl.BlockSpec((tm, d), lambda i, j: (i, 0)),
        out_shape=jax.ShapeDtypeStruct((m, d), F32),
        compiler_params=pltpu.CompilerParams(
            dimension_semantics=("parallel", "arbitrary"), vmem_limit_bytes=VMEM_LIMIT),
        name="mlp_ln",
    )(h, wu, wd, g, b)


def _pad_lanes(v):
    return jnp.pad(v.astype(F32), (0, LANE - v.shape[0]))[None, :]


def kernel(x, mem, w_in, conv_w, a_log, dt_bias, gdn_norm_w, pool_w, pool_scale, w_out, ln1_g, ln1_b,
           xq_w, xk_w, xv_w, xo_w, ln2_g, ln2_b, w_up, w_down, ln3_g, ln3_b):
    bsz, t, d = x.shape
    depth = w_in.shape[0]
    heads = a_log.shape[1]
    gw = conv_w.shape[2] // 3
    pw = pool_w.shape[1] * pool_w.shape[2]
    assert gw == heads * LANE and t % GDN_CHUNK == 0 and heads <= LANE
    alpha = (2.0 * depth) ** 0.25
    row2 = lambda v: v[None, :]

    h = x
    for l in range(depth):
        w16 = w_in[l].astype(BF16)
        zpad = jnp.zeros((d, LANE - heads), BF16)
        w_tail = jnp.concatenate(
            [w16[:, 4 * gw + 2 * heads:], w16[:, 4 * gw:4 * gw + heads], zpad,
             w16[:, 4 * gw + heads:4 * gw + 2 * heads], zpad], axis=1)
        assert pw % (2 * LANE) == 0

        proj, tail = _inproj(h.reshape(bsz * t, d), w16, 4 * gw, w_tail, tm=INPROJ_TM, tn=gw, sectioned=True)
        proj = proj.reshape(4, bsz, t, gw)
        tail = tail.reshape(bsz, t, pw + 2 * LANE)
        beta, gc = _gates(tail, _pad_lanes(a_log[l]), _pad_lanes(dt_bias[l]), pw // (2 * LANE))
        o_gdn = _gdn(proj, beta, gc, conv_w[l], row2(gdn_norm_w[l]), heads,
                     hb=GDN_HEADS_PER_STEP, cg=GDN_CHUNKS_PER_GROUP)
        o_pool, wo16, xk16, xv16 = _pool(tail, pool_w[l].astype(BF16), row2(pool_scale[l]), 0,
                                         casts=(w_out[l], xk_w[l], xv_w[l]))

        h1, xq16, xo16 = _outproj(o_gdn.reshape(bsz * t, gw), o_pool.reshape(bsz * t, pw),
                                  h.reshape(bsz * t, d), wo16, row2(ln1_g[l]), row2(ln1_b[l]), alpha,
                                  tm=OUTPROJ_TM, rb=OUTPROJ_RB, casts=(xq_w[l], xo_w[l]))

        mlen = mem.shape[1]
        mk, mv = _inproj(mem.reshape(bsz * mlen, d), xk16, d, xv16, tm=INPROJ_TM, tn=KVPROJ_TN, out_dtype=BF16)
        h2, wu16, wd16 = _xattn(h1.reshape(bsz, t, d), xq16, mk.reshape(bsz, mlen, d), mv.reshape(bsz, mlen, d),
                                xo16, row2(ln2_g[l]), row2(ln2_b[l]), alpha, tm=XATTN_TM, rb=XATTN_RB,
                                casts=(w_up[l], w_down[l]))

        h3 = _mlp(h2.reshape(bsz * t, d), wu16, wd16,
                  row2(ln3_g[l]), row2(ln3_b[l]), alpha, tm=MLP_TM, tf=MLP_TF, rb=MLP_RB)
        h = h3.reshape(bsz, t, d)
    return h
```

```python
import functools

import jax
import jax.numpy as jnp
from jax import lax
from jax.experimental import pallas as pl
from jax.experimental.pallas import tpu as pltpu

F32 = jnp.float32
BF16 = jnp.bfloat16

LANE = 128
GDN_CHUNK = 128
POOL_WINDOWS = (2, 4, 8, 16)
XATTN_HEADS = 4
LN_EPS = 1e-5
NORM_EPS = 1e-6
LOG2E = 1.4426950408889634
SUBLANE = 8
VMEM_LIMIT = 60 * 1024 * 1024
CONV_HALO = SUBLANE
POOL_HALO = 2 * SUBLANE

INPROJ_TM = 1024
KVPROJ_TN = 1024
OUTPROJ_TM, OUTPROJ_RB = 1024, 256
XATTN_TM, XATTN_RB = 512, 256
MLP_TM, MLP_TF, MLP_RB = 1024, 1024, 256
GDN_HEADS_PER_STEP = 2
GDN_CHUNKS_PER_GROUP = 4
POOL_ROWS = 2048

_NT = (((1,), (1,)), ((), ()))


def _dot(a, b):
    return jnp.dot(a, b, preferred_element_type=F32)


def _layer_norm(y, g, b):
    mu = jnp.mean(y, axis=-1, keepdims=True)
    yc = y - mu
    var = jnp.mean(yc * yc, axis=-1, keepdims=True)
    return yc * lax.rsqrt(var + LN_EPS) * g + b


def _silu(x):
    return x * jax.nn.sigmoid(x)


def _cast_specs(mats, nsteps, step_of):
    ins, outs, shapes = [], [], []
    for w in mats:
        r, c = w.shape
        assert r % (16 * nsteps) == 0
        spec = pl.BlockSpec((r // nsteps, c), lambda *ids: (step_of(*ids), 0))
        ins.append(spec)
        outs.append(spec)
        shapes.append(jax.ShapeDtypeStruct((r, c), BF16))
    return ins, outs, shapes


def _do_casts(src_refs, dst_refs):
    for s, d in zip(src_refs, dst_refs):
        d[...] = s[...].astype(d.dtype)


def _inproj_kernel(x_ref, wm_ref, wt_ref, om_ref, ot_ref, xb_ref, *, nmain):
    j = pl.program_id(1)

    @pl.when(j == 0)
    def _():
        xb_ref[...] = x_ref[...].astype(BF16)

    @pl.when(j < nmain)
    def _():
        om_ref[...] = _dot(xb_ref[...], wm_ref[...]).astype(om_ref.dtype)

    @pl.when(j == nmain)
    def _():
        ot_ref[...] = _dot(xb_ref[...], wt_ref[...]).astype(ot_ref.dtype)


def _inproj(x, w_all, n_main, w_tail, tm, tn, out_dtype=F32, sectioned=False):
    m, k = x.shape
    nt = w_tail.shape[1]
    nmain = n_main // tn
    main_idx = lambda j: jnp.minimum(j, nmain - 1)
    if sectioned:
        main_spec = pl.BlockSpec((None, tm, tn), lambda i, j: (main_idx(j), i, 0))
        main_shape = jax.ShapeDtypeStruct((nmain, m, tn), out_dtype)
    else:
        main_spec = pl.BlockSpec((tm, tn), lambda i, j: (i, main_idx(j)))
        main_shape = jax.ShapeDtypeStruct((m, n_main), out_dtype)
    return pl.pallas_call(
        functools.partial(_inproj_kernel, nmain=nmain),
        grid=(m // tm, nmain + 1),
        in_specs=[pl.BlockSpec((tm, k), lambda i, j: (i, 0)),
                  pl.BlockSpec((k, tn), lambda i, j: (0, main_idx(j))),
                  pl.BlockSpec((k, nt), lambda i, j: (0, 0), pipeline_mode=pl.Buffered(1))],
        out_specs=[main_spec, pl.BlockSpec((tm, nt), lambda i, j: (i, 0))],
        out_shape=[main_shape, jax.ShapeDtypeStruct((m, nt), out_dtype)],
        scratch_shapes=[pltpu.VMEM((tm, k), BF16)],
        compiler_params=pltpu.CompilerParams(
            dimension_semantics=("parallel", "arbitrary"), vmem_limit_bytes=VMEM_LIMIT),
        name="inproj",
    )(x, w_all, w_tail)


def _gates_kernel(ba_ref, alog_ref, dtb_ref, beta_ref, gc_ref):
    t = ba_ref.shape[1]
    c = GDN_CHUNK
    beta_ref[0] = jax.nn.sigmoid(ba_ref[0, :, :LANE])
    g = -jnp.exp(alog_ref[...]) * jax.nn.softplus(ba_ref[0, :, LANE:] + dtb_ref[...])
    row = lax.broadcasted_iota(jnp.int32, (c, c), 0)
    col = lax.broadcasted_iota(jnp.int32, (c, c), 1)
    tri = (row >= col).astype(F32)
    for i in range(t // c):
        gc_ref[0, i * c:(i + 1) * c, :] = LOG2E * jnp.dot(
            tri, g[i * c:(i + 1) * c], precision=lax.Precision.HIGHEST, preferred_element_type=F32)


def _gates(proj, alog_pad, dtb_pad, ba_block):
    b, t, _ = proj.shape
    return pl.pallas_call(
        _gates_kernel,
        grid=(b,),
        in_specs=[pl.BlockSpec((1, t, 2 * LANE), lambda i: (i, 0, ba_block)),
                  pl.BlockSpec((1, LANE), lambda i: (0, 0)),
                  pl.BlockSpec((1, LANE), lambda i: (0, 0))],
        out_specs=[pl.BlockSpec((1, t, LANE), lambda i: (i, 0, 0)),
                   pl.BlockSpec((1, t, LANE), lambda i: (i, 0, 0))],
        out_shape=[jax.ShapeDtypeStruct((b, t, LANE), F32)] * 2,
        compiler_params=pltpu.CompilerParams(dimension_semantics=("parallel",)),
        name="gdn_gates",
    )(proj, alog_pad, dtb_pad)


def _gdn_kernel(q_ref, k_ref, v_ref, z_ref, beta_ref, gc_ref, cwq_ref, cwk_ref, cwv_ref, nw_ref,
                o_ref, pad_ref, u_s, wq_s, ak_s, dl_s, *, hb, cg, scale):
    t = q_ref.shape[1]
    c = GDN_CHUNK
    nc = t // c
    conv_k = cwq_ref.shape[0]
    halo = CONV_HALO
    assert halo >= conv_k - 1
    jh = pl.program_id(1)

    for slot, src in enumerate((q_ref, k_ref, v_ref)):
        for s in range(hb):
            pad_ref[slot * hb + s, 0:halo, :] = jnp.zeros((halo, LANE), F32)
            pad_ref[slot * hb + s, halo:halo + t, :] = src[0, :, s * LANE:(s + 1) * LANE]

    row = lax.broadcasted_iota(jnp.int32, (c, c), 0)
    col = lax.broadcasted_iota(jnp.int32, (c, c), 1)
    incl = row >= col
    strict = row > col
    eye = (row == col).astype(F32)
    lane = lax.broadcasted_iota(jnp.int32, (c, LANE), 1)

    n_sq = (c - 1).bit_length() - 1
    n_prep = hb * cg * (4 + n_sq)
    n_seq = cg * (2 + hb)

    def prep_group(g, sb):
        st = []
        for s, cc in [(s, cc) for s in range(hb) for cc in range(cg)]:
            ci = g * cg + cc
            r0 = pl.multiple_of(ci * c, c)
            sl = slice(s * LANE, (s + 1) * LANE)

            def conv_act(slot, cw_ref):
                slab = slot * hb + s
                acc = cw_ref[0:1, sl] * pad_ref[slab, pl.ds(r0 + halo - conv_k + 1, c), :]
                for j in range(1, conv_k):
                    acc = acc + cw_ref[j:j + 1, sl] * pad_ref[slab, pl.ds(r0 + halo - conv_k + 1 + j, c), :]
                return _silu(acc)

            q = conv_act(0, cwq_ref)
            k = conv_act(1, cwk_ref)
            v = conv_act(2, cwv_ref)
            q = q * lax.rsqrt(jnp.sum(q * q, axis=-1, keepdims=True) + NORM_EPS) * scale
            k = k * lax.rsqrt(jnp.sum(k * k, axis=-1, keepdims=True) + NORM_EPS)

            hm = lane == (jh * hb + s)
            bcol = jnp.sum(jnp.where(hm, beta_ref[0, pl.ds(r0, c), :], 0.0), axis=1, keepdims=True)
            gcol = jnp.sum(jnp.where(hm, gc_ref[0, pl.ds(r0, c), :], 0.0), axis=1, keepdims=True)
            gcb = jnp.broadcast_to(gcol, (c, LANE))
            gct = gcb.T
            e = jnp.exp2(jnp.where(incl, gcb - gct, 0.0))
            ecol = jnp.exp2(gcol)
            kb = k * bcol
            glast = gcb[c - 1:c, :]
            dl_s[sb, s, cc] = jnp.broadcast_to(jnp.exp2(glast), (SUBLANE, LANE))
            st.append(dict(
                s=s, cc=cc, dec=jnp.where(incl, e, 0.0), ndec=jnp.where(strict, -e, 0.0),
                kbq=jnp.concatenate([kb, q], axis=0).astype(BF16), k16=k.astype(BF16),
                rhs=jnp.concatenate([v * bcol, kb * ecol], axis=1).astype(BF16),
                qg=(q * ecol).astype(BF16),
                kdt=(k * jnp.exp2(glast - gcb)).T.astype(BF16)))
            yield

        for d in st:
            kq = lax.dot_general(d["kbq"], d["k16"], _NT, preferred_element_type=F32)
            d["attn"] = (kq[c:] * d["dec"]).astype(BF16)
            d["x"] = kq[:c] * d["ndec"]
            d["p"] = eye + d["x"]
            yield
        for d in st:
            x16 = d["x"].astype(BF16)
            d["x"] = _dot(x16, x16)
            yield
        for n in range(1, n_sq + 1):
            for d in st:
                x16 = d["x"].astype(BF16)
                if n < n_sq:
                    r = _dot(jnp.concatenate([x16, d["p"].astype(BF16)], axis=0), x16)
                    d["x"] = r[:c]
                    d["p"] = d["p"] + r[c:]
                else:
                    d["p"] = d["p"] + _dot(d["p"].astype(BF16), x16)
                yield
        for d in st:
            s, cc = d["s"], d["cc"]
            uw = _dot(d["p"].astype(BF16), d["rhs"])
            u_s[sb, s, cc] = uw[:, :LANE]
            wq_s[sb, s, cc] = jnp.concatenate([uw[:, LANE:].astype(BF16), d["qg"]], axis=0)
            ak_s[sb, s, cc] = jnp.concatenate([d["attn"], d["kdt"]], axis=0)
            yield

    def seq_group(g, sb, states):
        for cc in range(cg):
            r0 = pl.multiple_of((g * cg + cc) * c, c)
            rs = [_dot(wq_s[sb, s, cc], states[s].astype(BF16)) for s in range(hb)]
            yield
            vns = [(u_s[sb, s, cc] - rs[s][:c]).astype(BF16) for s in range(hb)]
            r2s = [_dot(ak_s[sb, s, cc], vns[s]) for s in range(hb)]
            yield
            for s in range(hb):
                sl = slice(s * LANE, (s + 1) * LANE)
                states[s] = states[s] * dl_s[sb, s, cc][0:1, :] + r2s[s][c:]
                o = rs[s][c:] + r2s[s][:c]
                z = z_ref[0, pl.ds(r0, c), sl]
                o = o * lax.rsqrt(jnp.mean(o * o, axis=-1, keepdims=True) + NORM_EPS) * nw_ref[...]
                o_ref[0, pl.ds(r0, c), sl] = (o * _silu(z)).astype(o_ref.dtype)
                yield

    def weave(seq, prep):
        done = 0
        for i in range(n_prep):
            while done * n_prep <= i * n_seq and done < n_seq:
                next(seq)
                done += 1
            next(prep)
        for _ in range(done, n_seq):
            next(seq)

    u_s[1] = jnp.zeros(u_s.shape[1:], u_s.dtype)
    wq_s[1] = jnp.zeros(wq_s.shape[1:], wq_s.dtype)
    ak_s[1] = jnp.zeros(ak_s.shape[1:], ak_s.dtype)
    dl_s[1] = jnp.zeros(dl_s.shape[1:], dl_s.dtype)
    ng = nc // cg

    def pipe_body(m, states):
        states = list(states)
        weave(seq_group(jnp.maximum(2 * m - 1, 0), 1, states), prep_group(2 * m, 0))
        weave(seq_group(2 * m, 0, states), prep_group(2 * m + 1, 1))
        return tuple(states)

    states = lax.fori_loop(0, ng // 2, pipe_body, tuple(jnp.zeros((LANE, LANE), F32) for _ in range(hb)))
    for _ in seq_group(ng - 1, 1, list(states)):
        pass


def _gdn(proj, beta, gc, conv_w, norm_w, heads, hb, cg):
    _, b, t, _ = proj.shape
    nj = heads // hb
    assert (t // GDN_CHUNK) % (2 * cg) == 0 and heads % hb == 0
    wblk = hb * LANE
    kern = functools.partial(_gdn_kernel, hb=hb, cg=cg, scale=float(LANE) ** -0.5)
    section = lambda n: pl.BlockSpec((None, 1, t, wblk), lambda i, j: (n, i, 0, j))
    return pl.pallas_call(
        kern,
        grid=(b, nj),
        in_specs=[section(0), section(1), section(2), section(3),
                  pl.BlockSpec((1, t, LANE), lambda i, j: (i, 0, 0)),
                  pl.BlockSpec((1, t, LANE), lambda i, j: (i, 0, 0)),
                  pl.BlockSpec((conv_w.shape[0], wblk), lambda i, j: (0, j)),
                  pl.BlockSpec((conv_w.shape[0], wblk), lambda i, j: (0, nj + j)),
                  pl.BlockSpec((conv_w.shape[0], wblk), lambda i, j: (0, 2 * nj + j)),
                  pl.BlockSpec((1, LANE), lambda i, j: (0, 0))],
        out_specs=pl.BlockSpec((1, t, wblk), lambda i, j: (i, 0, j)),
        out_shape=jax.ShapeDtypeStruct((b, t, heads * LANE), BF16),
        scratch_shapes=[pltpu.VMEM((3 * hb, t + CONV_HALO, LANE), F32),
                        pltpu.VMEM((2, hb, cg, GDN_CHUNK, LANE), F32),
                        pltpu.VMEM((2, hb, cg, 2 * GDN_CHUNK, LANE), BF16),
                        pltpu.VMEM((2, hb, cg, 2 * GDN_CHUNK, LANE), BF16),
                        pltpu.VMEM((2, hb, cg, SUBLANE, LANE), F32)],
        compiler_params=pltpu.CompilerParams(
            dimension_semantics=("parallel", "arbitrary"), vmem_limit_bytes=VMEM_LIMIT),
        name="gdn_delta_rule",
    )(proj, proj, proj, proj, beta, gc, conv_w, conv_w, conv_w, norm_w)


def _pool_kernel(*refs, rows, ncast):
    p_ref, w_ref, sc_ref = refs[:3]
    o_ref, pad_ref = refs[3 + ncast], refs[-1]
    _do_casts(refs[3:3 + ncast], refs[4 + ncast:-1])
    t = p_ref.shape[1]
    cgd = w_ref.shape[1]
    halo = POOL_HALO
    assert halo >= max(POOL_WINDOWS) - 1
    nslab = cgd // LANE
    for g, win in enumerate(POOL_WINDOWS):
        sl = slice(g * cgd, (g + 1) * cgd)
        for k in range(nslab):
            pad_ref[k, 0:halo, :] = jnp.zeros((halo, LANE), F32)
            pad_ref[k, halo:halo + t, :] = p_ref[0, :, g * cgd + k * LANE:g * cgd + (k + 1) * LANE]

        def body(i, carry, win=win, sl=sl, g=g):
            r0 = pl.multiple_of(i * rows, rows)

            def shifted(d):
                return jnp.concatenate(
                    [pad_ref[k, pl.ds(r0 + halo - d, rows), :] for k in range(nslab)], axis=1)

            x = shifted(0)
            ssum = x
            for d in range(1, win):
                ssum = ssum + shifted(d)
            pos = r0 + lax.broadcasted_iota(jnp.int32, (rows, cgd), 0)
            cnt = jnp.minimum(pos + 1, win).astype(F32)
            pooled = ssum / cnt - x
            mixed = _dot(pooled.astype(BF16), w_ref[g]) * sc_ref[:, sl]
            o_ref[0, pl.ds(r0, rows), sl] = mixed.astype(o_ref.dtype)
            return carry

        lax.fori_loop(0, t // rows, body, 0)


def _pool(proj, pool_w16, pool_scale, p_block, casts):
    b, t, _ = proj.shape
    g, cgd, _ = pool_w16.shape
    width = g * cgd
    cin, cout, cshape = _cast_specs(casts, b, lambda i: i)
    return pl.pallas_call(
        functools.partial(_pool_kernel, rows=POOL_ROWS, ncast=len(casts)),
        grid=(b,),
        in_specs=[pl.BlockSpec((1, t, width), lambda i: (i, 0, p_block)),
                  pl.BlockSpec((g, cgd, cgd), lambda i: (0, 0, 0)),
                  pl.BlockSpec((1, width), lambda i: (0, 0))] + cin,
        out_specs=[pl.BlockSpec((1, t, width), lambda i: (i, 0, 0))] + cout,
        out_shape=[jax.ShapeDtypeStruct((b, t, width), BF16)] + cshape,
        scratch_shapes=[pltpu.VMEM((cgd // LANE, t + POOL_HALO, LANE), F32)],
        compiler_params=pltpu.CompilerParams(
            dimension_semantics=("parallel",), vmem_limit_bytes=VMEM_LIMIT),
        name="multiscale_pool",
    )(proj, pool_w16, pool_scale, *casts)


def _outproj_kernel(*refs, alpha, rb, ncast):
    og_ref, op_ref, x_ref, wa_ref, wb_ref, g_ref, b_ref = refs[:7]
    o_ref = refs[7 + ncast]
    _do_casts(refs[7:7 + ncast], refs[8 + ncast:])
    for r in range(o_ref.shape[0] // rb):
        rows = slice(r * rb, (r + 1) * rb)
        mix = _dot(og_ref[rows, :], wa_ref[...]) + _dot(op_ref[rows, :], wb_ref[...])
        o_ref[rows, :] = _layer_norm(alpha * x_ref[rows, :] + mix, g_ref[...], b_ref[...])


def _outproj(og, op, x, w, g, b, alpha, tm, rb, casts):
    m, d = x.shape
    ka, kb = og.shape[1], op.shape[1]
    assert ka % kb == 0 and w.shape[0] == ka + kb
    const = lambda i: (0, 0)
    once = pl.Buffered(1)
    cin, cout, cshape = _cast_specs(casts, m // tm, lambda i: i)
    return pl.pallas_call(
        functools.partial(_outproj_kernel, alpha=alpha, rb=rb, ncast=len(casts)),
        grid=(m // tm,),
        in_specs=[pl.BlockSpec((tm, ka), lambda i: (i, 0)),
                  pl.BlockSpec((tm, kb), lambda i: (i, 0)),
                  pl.BlockSpec((tm, d), lambda i: (i, 0)),
                  pl.BlockSpec((ka, d), const, pipeline_mode=once),
                  pl.BlockSpec((kb, d), lambda i: (ka // kb, 0), pipeline_mode=once),
                  pl.BlockSpec((1, d), const),
                  pl.BlockSpec((1, d), const)] + cin,
        out_specs=[pl.BlockSpec((tm, d), lambda i: (i, 0))] + cout,
        out_shape=[jax.ShapeDtypeStruct((m, d), F32)] + cshape,
        compiler_params=pltpu.CompilerParams(
            dimension_semantics=("parallel",), vmem_limit_bytes=VMEM_LIMIT),
        name="outproj_ln",
    )(og, op, x, w, w, g, b, *casts)


def _xattn_kernel(*refs, alpha, heads, rb, ncast):
    h_ref, wq_ref, k_ref, v_ref, wo_ref, g_ref, b_ref = refs[:7]
    o_ref = refs[7 + ncast]
    _do_casts(refs[7:7 + ncast], refs[8 + ncast:])
    d = h_ref.shape[-1]
    hd = d // heads
    for r in range(h_ref.shape[1] // rb):
        rows = slice(r * rb, (r + 1) * rb)
        h = h_ref[0, rows, :]
        q = _dot(h.astype(BF16), wq_ref[...])
        outs = []
        for i in range(heads):
            qh = q[:, i * hd:(i + 1) * hd].astype(BF16)
            kh = k_ref[0, :, i * hd:(i + 1) * hd]
            vh = v_ref[0, :, i * hd:(i + 1) * hd]
            s = lax.dot_general(qh, kh, _NT, preferred_element_type=F32) * (float(hd) ** -0.5)
            e = jnp.exp(s - jnp.max(s, axis=-1, keepdims=True))
            p = e / jnp.sum(e, axis=-1, keepdims=True)
            outs.append(_dot(p.astype(BF16), vh).astype(BF16))
        xa = _dot(jnp.concatenate(outs, axis=1), wo_ref[...])
        o_ref[0, rows, :] = _layer_norm(alpha * h + xa, g_ref[...], b_ref[...])


def _xattn(h, wq, k, v, wo, g, b, alpha, tm, rb, casts):
    bsz, t, d = h.shape
    mlen = k.shape[1]
    nj = t // tm
    const = lambda i, j: (0, 0)
    once = pl.Buffered(1)
    cin, cout, cshape = _cast_specs(casts, bsz * nj, lambda i, j: i * nj + j)
    return pl.pallas_call(
        functools.partial(_xattn_kernel, alpha=alpha, heads=XATTN_HEADS, rb=rb, ncast=len(casts)),
        grid=(bsz, nj),
        in_specs=[pl.BlockSpec((1, tm, d), lambda i, j: (i, j, 0)),
                  pl.BlockSpec((d, d), const, pipeline_mode=once),
                  pl.BlockSpec((1, mlen, d), lambda i, j: (i, 0, 0)),
                  pl.BlockSpec((1, mlen, d), lambda i, j: (i, 0, 0)),
                  pl.BlockSpec((d, d), const, pipeline_mode=once),
                  pl.BlockSpec((1, d), const),
                  pl.BlockSpec((1, d), const)] + cin,
        out_specs=[pl.BlockSpec((1, tm, d), lambda i, j: (i, j, 0))] + cout,
        out_shape=[jax.ShapeDtypeStruct((bsz, t, d), F32)] + cshape,
        compiler_params=pltpu.CompilerParams(
            dimension_semantics=("parallel", "parallel"), vmem_limit_bytes=VMEM_LIMIT),
        name="xattn_ln",
    )(h, wq, k, v, wo, g, b, *casts)


def _mlp_kernel(h_ref, wu_ref, wd_ref, g_ref, b_ref, o_ref, *, alpha, rb):
    j = pl.program_id(1)
    last = pl.num_programs(1) - 1

    def ff(rows):
        u = jnp.maximum(_dot(h_ref[rows, :].astype(BF16), wu_ref[...]), 0.0)
        return _dot((u * u).astype(BF16), wd_ref[...])

    @pl.when(j == 0)
    def _():
        o_ref[...] = ff(slice(None))

    @pl.when(jnp.logical_and(j > 0, j < last))
    def _():
        o_ref[...] += ff(slice(None))

    @pl.when(j == last)
    def _():
        for r in range(o_ref.shape[0] // rb):
            rows = slice(r * rb, (r + 1) * rb)
            y = alpha * h_ref[rows, :] + (o_ref[rows, :] + ff(rows))
            o_ref[rows, :] = _layer_norm(y, g_ref[...], b_ref[...])


def _mlp(h, wu, wd, g, b, alpha, tm, tf, rb):
    m, d = h.shape
    f = wu.shape[1]
    assert f // tf >= 2
    const = lambda i, j: (0, 0)
    return pl.pallas_call(
        functools.partial(_mlp_kernel, alpha=alpha, rb=rb),
        grid=(m // tm, f // tf),
        in_specs=[pl.BlockSpec((tm, d), lambda i, j: (i, 0)),
                  pl.BlockSpec((d, tf), lambda i, j: (0, j)),
                  pl.BlockSpec((tf, d), lambda i, j: (j, 0)),
                  pl.BlockSpec((1, d), const),
                  pl.BlockSpec((1, d), const)],
        out_specs=pl.BlockSpec((tm, d), lambda i, j: (i, 0)),
        out_shape=jax.ShapeDtypeStruct((m, d), F32),
        compiler_params=pltpu.CompilerParams(
            dimension_semantics=("parallel", "arbitrary"), vmem_limit_bytes=VMEM_LIMIT),
        name="mlp_ln",
    )(h, wu, wd, g, b)


def _pad_lanes(v):
    return jnp.pad(v.astype(F32), (0, LANE - v.shape[0]))[None, :]


def kernel(x, mem, w_in, conv_w, a_log, dt_bias, gdn_norm_w, pool_w, pool_scale, w_out, ln1_g, ln1_b,
           xq_w, xk_w, xv_w, xo_w, ln2_g, ln2_b, w_up, w_down, ln3_g, ln3_b):
    bsz, t, d = x.shape
    depth = w_in.shape[0]
    heads = a_log.shape[1]
    gw = conv_w.shape[2] // 3
    pw = pool_w.shape[1] * pool_w.shape[2]
    assert gw == heads * LANE and t % GDN_CHUNK == 0 and heads <= LANE
    alpha = (2.0 * depth) ** 0.25
    row2 = lambda v: v[None, :]

    h = x
    for l in range(depth):
        w16 = w_in[l].astype(BF16)
        zpad = jnp.zeros((d, LANE - heads), BF16)
        w_tail = jnp.concatenate(
            [w16[:, 4 * gw + 2 * heads:], w16[:, 4 * gw:4 * gw + heads], zpad,
             w16[:, 4 * gw + heads:4 * gw + 2 * heads], zpad], axis=1)
        assert pw % (2 * LANE) == 0

        proj, tail = _inproj(h.reshape(bsz * t, d), w16, 4 * gw, w_tail, tm=INPROJ_TM, tn=gw, sectioned=True)
        proj = proj.reshape(4, bsz, t, gw)
        tail = tail.reshape(bsz, t, pw + 2 * LANE)
        beta, gc = _gates(tail, _pad_lanes(a_log[l]), _pad_lanes(dt_bias[l]), pw // (2 * LANE))
        o_gdn = _gdn(proj, beta, gc, conv_w[l], row2(gdn_norm_w[l]), heads,
                     hb=GDN_HEADS_PER_STEP, cg=GDN_CHUNKS_PER_GROUP)
        o_pool, wo16, xk16, xv16 = _pool(tail, pool_w[l].astype(BF16), row2(pool_scale[l]), 0,
                                         casts=(w_out[l], xk_w[l], xv_w[l]))

        h1, xq16, xo16 = _outproj(o_gdn.reshape(bsz * t, gw), o_pool.reshape(bsz * t, pw),
                                  h.reshape(bsz * t, d), wo16, row2(ln1_g[l]), row2(ln1_b[l]), alpha,
                                  tm=OUTPROJ_TM, rb=OUTPROJ_RB, casts=(xq_w[l], xo_w[l]))

        mlen = mem.shape[1]
        mk, mv = _inproj(mem.reshape(bsz * mlen, d), xk16, d, xv16, tm=INPROJ_TM, tn=KVPROJ_TN, out_dtype=BF16)
        h2, wu16, wd16 = _xattn(h1.reshape(bsz, t, d), xq16, mk.reshape(bsz, mlen, d), mv.reshape(bsz, mlen, d),
                                xo16, row2(ln2_g[l]), row2(ln2_b[l]), alpha, tm=XATTN_TM, rb=XATTN_RB,
                                casts=(w_up[l], w_down[l]))

        h3 = _mlp(h2.reshape(bsz * t, d), wu16, wd16,
                  row2(ln3_g[l]), row2(ln3_b[l]), alpha, tm=MLP_TM, tf=MLP_TF, rb=MLP_RB)
        h = h3.reshape(bsz, t, d)
    return h
```

```python
import functools

import jax
import jax.numpy as jnp
from jax import lax
from jax.experimental import pallas as pl
from jax.experimental.pallas import tpu as pltpu

F32 = jnp.float32
BF16 = jnp.bfloat16

LANE = 128
GDN_CHUNK = 128
POOL_WINDOWS = (2, 4, 8, 16)
XATTN_HEADS = 4
LN_EPS = 1e-5
NORM_EPS = 1e-6
LOG2E = 1.4426950408889634
SUBLANE = 8
VMEM_LIMIT = 60 * 1024 * 1024
CONV_HALO = SUBLANE
POOL_HALO = 2 * SUBLANE

INPROJ_TM = 1024
KVPROJ_TN = 1024
OUTPROJ_TM, OUTPROJ_RB = 1024, 256
XATTN_TM, XATTN_RB = 512, 256
MLP_TM, MLP_TF, MLP_RB = 1024, 1024, 256
GDN_HEADS_PER_STEP = 2
GDN_CHUNKS_PER_GROUP = 4
POOL_ROWS = 2048

_NT = (((1,), (1,)), ((), ()))


def _dot(a, b):
    return jnp.dot(a, b, preferred_element_type=F32)


def _layer_norm(y, g, b):
    mu = jnp.mean(y, axis=-1, keepdims=True)
    yc = y - mu
    var = jnp.mean(yc * yc, axis=-1, keepdims=True)
    return yc * lax.rsqrt(var + LN_EPS) * g + b


def _silu(x):
    return x * jax.nn.sigmoid(x)


def _cast_specs(mats, nsteps, step_of):
    ins, outs, shapes = [], [], []
    for w in mats:
        r, c = w.shape
        assert r % (16 * nsteps) == 0
        spec = pl.BlockSpec((r // nsteps, c), lambda *ids: (step_of(*ids), 0))
        ins.append(spec)
        outs.append(spec)
        shapes.append(jax.ShapeDtypeStruct((r, c), BF16))
    return ins, outs, shapes


def _do_casts(src_refs, dst_refs):
    for s, d in zip(src_refs, dst_refs):
        d[...] = s[...].astype(d.dtype)


def _inproj_kernel(x_ref, wm_ref, wt_ref, om_ref, ot_ref, xb_ref, *, nmain):
    j = pl.program_id(1)

    @pl.when(j == 0)
    def _():
        xb_ref[...] = x_ref[...].astype(BF16)

    @pl.when(j < nmain)
    def _():
        om_ref[...] = _dot(xb_ref[...], wm_ref[...]).astype(om_ref.dtype)

    @pl.when(j == nmain)
    def _():
        ot_ref[...] = _dot(xb_ref[...], wt_ref[...]).astype(ot_ref.dtype)


def _inproj(x, w_all, n_main, w_tail, tm, tn, out_dtype=F32, sectioned=False):
    m, k = x.shape
    nt = w_tail.shape[1]
    nmain = n_main // tn
    main_idx = lambda j: jnp.minimum(j, nmain - 1)
    if sectioned:
        main_spec = pl.BlockSpec((None, tm, tn), lambda i, j: (main_idx(j), i, 0))
        main_shape = jax.ShapeDtypeStruct((nmain, m, tn), out_dtype)
    else:
        main_spec = pl.BlockSpec((tm, tn), lambda i, j: (i, main_idx(j)))
        main_shape = jax.ShapeDtypeStruct((m, n_main), out_dtype)
    return pl.pallas_call(
        functools.partial(_inproj_kernel, nmain=nmain),
        grid=(m // tm, nmain + 1),
        in_specs=[pl.BlockSpec((tm, k), lambda i, j: (i, 0)),
                  pl.BlockSpec((k, tn), lambda i, j: (0, main_idx(j))),
                  pl.BlockSpec((k, nt), lambda i, j: (0, 0), pipeline_mode=pl.Buffered(1))],
        out_specs=[main_spec, pl.BlockSpec((tm, nt), lambda i, j: (i, 0))],
        out_shape=[main_shape, jax.ShapeDtypeStruct((m, nt), out_dtype)],
        scratch_shapes=[pltpu.VMEM((tm, k), BF16)],
        compiler_params=pltpu.CompilerParams(
            dimension_semantics=("parallel", "arbitrary"), vmem_limit_bytes=VMEM_LIMIT),
        name="inproj",
    )(x, w_all, w_tail)


def _gates_kernel(ba_ref, alog_ref, dtb_ref, beta_ref, gc_ref):
    t = ba_ref.shape[1]
    c = GDN_CHUNK
    beta_ref[0] = jax.nn.sigmoid(ba_ref[0, :, :LANE])
    g = -jnp.exp(alog_ref[...]) * jax.nn.softplus(ba_ref[0, :, LANE:] + dtb_ref[...])
    row = lax.broadcasted_iota(jnp.int32, (c, c), 0)
    col = lax.broadcasted_iota(jnp.int32, (c, c), 1)
    tri = (row >= col).astype(F32)
    for i in range(t // c):
        gc_ref[0, i * c:(i + 1) * c, :] = LOG2E * jnp.dot(
            tri, g[i * c:(i + 1) * c], precision=lax.Precision.HIGHEST, preferred_element_type=F32)


def _gates(proj, alog_pad, dtb_pad, ba_block):
    b, t, _ = proj.shape
    return pl.pallas_call(
        _gates_kernel,
        grid=(b,),
        in_specs=[pl.BlockSpec((1, t, 2 * LANE), lambda i: (i, 0, ba_block)),
                  pl.BlockSpec((1, LANE), lambda i: (0, 0)),
                  pl.BlockSpec((1, LANE), lambda i: (0, 0))],
        out_specs=[pl.BlockSpec((1, t, LANE), lambda i: (i, 0, 0)),
                   pl.BlockSpec((1, t, LANE), lambda i: (i, 0, 0))],
        out_shape=[jax.ShapeDtypeStruct((b, t, LANE), F32)] * 2,
        compiler_params=pltpu.CompilerParams(dimension_semantics=("parallel",)),
        name="gdn_gates",
    )(proj, alog_pad, dtb_pad)


def _gdn_kernel(q_ref, k_ref, v_ref, z_ref, beta_ref, gc_ref, cwq_ref, cwk_ref, cwv_ref, nw_ref,
                o_ref, pad_ref, u_s, wq_s, ak_s, dl_s, *, hb, cg, scale):
    t = q_ref.shape[1]
    c = GDN_CHUNK
    nc = t // c
    conv_k = cwq_ref.shape[0]
    halo = CONV_HALO
    assert halo >= conv_k - 1
    jh = pl.program_id(1)

    for slot, src in enumerate((q_ref, k_ref, v_ref)):
        for s in range(hb):
            pad_ref[slot * hb + s, 0:halo, :] = jnp.zeros((halo, LANE), F32)
            pad_ref[slot * hb + s, halo:halo + t, :] = src[0, :, s * LANE:(s + 1) * LANE]

    row = lax.broadcasted_iota(jnp.int32, (c, c), 0)
    col = lax.broadcasted_iota(jnp.int32, (c, c), 1)
    incl = row >= col
    strict = row > col
    eye = (row == col).astype(F32)
    lane = lax.broadcasted_iota(jnp.int32, (c, LANE), 1)

    n_sq = (c - 1).bit_length() - 1
    n_prep = hb * cg * (4 + n_sq)
    n_seq = cg * (2 + hb)

    def prep_group(g, sb):
        st = []
        for s, cc in [(s, cc) for s in range(hb) for cc in range(cg)]:
            ci = g * cg + cc
            r0 = pl.multiple_of(ci * c, c)
            sl = slice(s * LANE, (s + 1) * LANE)

            def conv_act(slot, cw_ref):
                slab = slot * hb + s
                acc = cw_ref[0:1, sl] * pad_ref[slab, pl.ds(r0 + halo - conv_k + 1, c), :]
                for j in range(1, conv_k):
                    acc = acc + cw_ref[j:j + 1, sl] * pad_ref[slab, pl.ds(r0 + halo - conv_k + 1 + j, c), :]
                return _silu(acc)

            q = conv_act(0, cwq_ref)
            k = conv_act(1, cwk_ref)
            v = conv_act(2, cwv_ref)
            q = q * lax.rsqrt(jnp.sum(q * q, axis=-1, keepdims=True) + NORM_EPS) * scale
            k = k * lax.rsqrt(jnp.sum(k * k, axis=-1, keepdims=True) + NORM_EPS)

            hm = lane == (jh * hb + s)
            bcol = jnp.sum(jnp.where(hm, beta_ref[0, pl.ds(r0, c), :], 0.0), axis=1, keepdims=True)
            gcol = jnp.sum(jnp.where(hm, gc_ref[0, pl.ds(r0, c), :], 0.0), axis=1, keepdims=True)
            gcb = jnp.broadcast_to(gcol, (c, LANE))
            gct = gcb.T
            e = jnp.exp2(jnp.where(incl, gcb - gct, 0.0))
            ecol = jnp.exp2(gcol)
            kb = k * bcol
            glast = gcb[c - 1:c, :]
            dl_s[sb, s, cc] = jnp.broadcast_to(jnp.exp2(glast), (SUBLANE, LANE))
            st.append(dict(
                s=s, cc=cc, dec=jnp.where(incl, e, 0.0), ndec=jnp.where(strict, -e, 0.0),
                kbq=jnp.concatenate([kb, q], axis=0).astype(BF16), k16=k.astype(BF16),
                rhs=jnp.concatenate([v * bcol, kb * ecol], axis=1).astype(BF16),
                qg=(q * ecol).astype(BF16),
                kdt=(k * jnp.exp2(glast - gcb)).T.astype(BF16)))
            yield

        for d in st:
            kq = lax.dot_general(d["kbq"], d["k16"], _NT, preferred_element_type=F32)
            d["attn"] = (kq[c:] * d["dec"]).astype(BF16)
            d["x"] = kq[:c] * d["ndec"]
            d["p"] = eye + d["x"]
            yield
        for d in st:
            x16 = d["x"].astype(BF16)
            d["x"] = _dot(x16, x16)
            yield
        for n in range(1, n_sq + 1):
            for d in st:
                x16 = d["x"].astype(BF16)
                if n < n_sq:
                    r = _dot(jnp.concatenate([x16, d["p"].astype(BF16)], axis=0), x16)
                    d["x"] = r[:c]
                    d["p"] = d["p"] + r[c:]
                else:
                    d["p"] = d["p"] + _dot(d["p"].astype(BF16), x16)
                yield
        for d in st:
            s, cc = d["s"], d["cc"]
            uw = _dot(d["p"].astype(BF16), d["rhs"])
            u_s[sb, s, cc] = uw[:, :LANE]
            wq_s[sb, s, cc] = jnp.concatenate([uw[:, LANE:].astype(BF16), d["qg"]], axis=0)
            ak_s[sb, s, cc] = jnp.concatenate([d["attn"], d["kdt"]], axis=0)
            yield

    def seq_group(g, sb, states):
        for cc in range(cg):
            r0 = pl.multiple_of((g * cg + cc) * c, c)
            rs = [_dot(wq_s[sb, s, cc], states[s].astype(BF16)) for s in range(hb)]
            yield
            vns = [(u_s[sb, s, cc] - rs[s][:c]).astype(BF16) for s in range(hb)]
            r2s = [_dot(ak_s[sb, s, cc], vns[s]) for s in range(hb)]
            yield
            for s in range(hb):
                sl = slice(s * LANE, (s + 1) * LANE)
                states[s] = states[s] * dl_s[sb, s, cc][0:1, :] + r2s[s][c:]
                o = rs[s][c:] + r2s[s][:c]
                z = z_ref[0, pl.ds(r0, c), sl]
                o = o * lax.rsqrt(jnp.mean(o * o, axis=-1, keepdims=True) + NORM_EPS) * nw_ref[...]
                o_ref[0, pl.ds(r0, c), sl] = (o * _silu(z)).astype(o_ref.dtype)
                yield

    def weave(seq, prep):
        done = 0
        for i in range(n_prep):
            while done * n_prep <= i * n_seq and done < n_seq:
                next(seq)
                done += 1
            next(prep)
        for _ in range(done, n_seq):
            next(seq)

    u_s[1] = jnp.zeros(u_s.shape[1:], u_s.dtype)
    wq_s[1] = jnp.zeros(wq_s.shape[1:], wq_s.dtype)
    ak_s[1] = jnp.zeros(ak_s.shape[1:], ak_s.dtype)
    dl_s[1] = jnp.zeros(dl_s.shape[1:], dl_s.dtype)
    ng = nc // cg

    def pipe_body(m, states):
        states = list(states)
        weave(seq_group(jnp.maximum(2 * m - 1, 0), 1, states), prep_group(2 * m, 0))
        weave(seq_group(2 * m, 0, states), prep_group(2 * m + 1, 1))
        return tuple(states)

    states = lax.fori_loop(0, ng // 2, pipe_body, tuple(jnp.zeros((LANE, LANE), F32) for _ in range(hb)))
    for _ in seq_group(ng - 1, 1, list(states)):
        pass


def _gdn(proj, beta, gc, conv_w, norm_w, heads, hb, cg):
    _, b, t, _ = proj.shape
    nj = heads // hb
    assert (t // GDN_CHUNK) % (2 * cg) == 0 and heads % hb == 0
    wblk = hb * LANE
    kern = functools.partial(_gdn_kernel, hb=hb, cg=cg, scale=float(LANE) ** -0.5)
    section = lambda n: pl.BlockSpec((None, 1, t, wblk), lambda i, j: (n, i, 0, j))
    return pl.pallas_call(
        kern,
        grid=(b, nj),
        in_specs=[section(0), section(1), section(2), section(3),
                  pl.BlockSpec((1, t, LANE), lambda i, j: (i, 0, 0)),
                  pl.BlockSpec((1, t, LANE), lambda i, j: (i, 0, 0)),
                  pl.BlockSpec((conv_w.shape[0], wblk), lambda i, j: (0, j)),
                  pl.BlockSpec((conv_w.shape[0], wblk), lambda i, j: (0, nj + j)),
                  pl.BlockSpec((conv_w.shape[0], wblk), lambda i, j: (0, 2 * nj + j)),
                  pl.BlockSpec((1, LANE), lambda i, j: (0, 0))],
        out_specs=pl.BlockSpec((1, t, wblk), lambda i, j: (i, 0, j)),
        out_shape=jax.ShapeDtypeStruct((b, t, heads * LANE), BF16),
        scratch_shapes=[pltpu.VMEM((3 * hb, t + CONV_HALO, LANE), F32),
                        pltpu.VMEM((2, hb, cg, GDN_CHUNK, LANE), F32),
                        pltpu.VMEM((2, hb, cg, 2 * GDN_CHUNK, LANE), BF16),
                        pltpu.VMEM((2, hb, cg, 2 * GDN_CHUNK, LANE), BF16),
                        pltpu.VMEM((2, hb, cg, SUBLANE, LANE), F32)],
        compiler_params=pltpu.CompilerParams(
            dimension_semantics=("parallel", "arbitrary"), vmem_limit_bytes=VMEM_LIMIT),
        name="gdn_delta_rule",
    )(proj, proj, proj, proj, beta, gc, conv_w, conv_w, conv_w, norm_w)


def _pool_kernel(*refs, rows, ncast):
    p_ref, w_ref, sc_ref, ba_ref, alog_ref, dtb_ref = refs[:6]
    o_ref, beta_ref, gc_ref = refs[6 + ncast:9 + ncast]
    pad_ref = refs[-1]
    _do_casts(refs[6:6 + ncast], refs[9 + ncast:-1])
    _gates_kernel(ba_ref, alog_ref, dtb_ref, beta_ref, gc_ref)
    t = p_ref.shape[1]
    cgd = w_ref.shape[1]
    halo = POOL_HALO
    assert halo >= max(POOL_WINDOWS) - 1
    nslab = cgd // LANE
    for g, win in enumerate(POOL_WINDOWS):
        sl = slice(g * cgd, (g + 1) * cgd)
        for k in range(nslab):
            pad_ref[k, 0:halo, :] = jnp.zeros((halo, LANE), F32)
            pad_ref[k, halo:halo + t, :] = p_ref[0, :, g * cgd + k * LANE:g * cgd + (k + 1) * LANE]

        def body(i, carry, win=win, sl=sl, g=g):
            r0 = pl.multiple_of(i * rows, rows)

            def shifted(d):
                return jnp.concatenate(
                    [pad_ref[k, pl.ds(r0 + halo - d, rows), :] for k in range(nslab)], axis=1)

            x = shifted(0)
            ssum = x
            for d in range(1, win):
                ssum = ssum + shifted(d)
            pos = r0 + lax.broadcasted_iota(jnp.int32, (rows, cgd), 0)
            cnt = jnp.minimum(pos + 1, win).astype(F32)
            pooled = ssum / cnt - x
            mixed = _dot(pooled.astype(BF16), w_ref[g]) * sc_ref[:, sl]
            o_ref[0, pl.ds(r0, rows), sl] = mixed.astype(o_ref.dtype)
            return carry

        lax.fori_loop(0, t // rows, body, 0)


def _pool(proj, pool_w16, pool_scale, p_block, alog_pad, dtb_pad, ba_block, casts):
    b, t, _ = proj.shape
    g, cgd, _ = pool_w16.shape
    width = g * cgd
    assert t % POOL_ROWS == 0
    cin, cout, cshape = _cast_specs(casts, b, lambda i: i)
    gate_spec = pl.BlockSpec((1, t, LANE), lambda i: (i, 0, 0))
    gate_shape = jax.ShapeDtypeStruct((b, t, LANE), F32)
    return pl.pallas_call(
        functools.partial(_pool_kernel, rows=POOL_ROWS, ncast=len(casts)),
        grid=(b,),
        in_specs=[pl.BlockSpec((1, t, width), lambda i: (i, 0, p_block)),
                  pl.BlockSpec((g, cgd, cgd), lambda i: (0, 0, 0)),
                  pl.BlockSpec((1, width), lambda i: (0, 0)),
                  pl.BlockSpec((1, t, 2 * LANE), lambda i: (i, 0, ba_block)),
                  pl.BlockSpec((1, LANE), lambda i: (0, 0)),
                  pl.BlockSpec((1, LANE), lambda i: (0, 0))] + cin,
        out_specs=[pl.BlockSpec((1, t, width), lambda i: (i, 0, 0)), gate_spec, gate_spec] + cout,
        out_shape=[jax.ShapeDtypeStruct((b, t, width), BF16), gate_shape, gate_shape] + cshape,
        scratch_shapes=[pltpu.VMEM((cgd // LANE, t + POOL_HALO, LANE), F32)],
        compiler_params=pltpu.CompilerParams(
            dimension_semantics=("parallel",), vmem_limit_bytes=VMEM_LIMIT),
        name="multiscale_pool",
    )(proj, pool_w16, pool_scale, proj, alog_pad, dtb_pad, *casts)


def _outproj_kernel(*refs, alpha, rb, ncast):
    og_ref, op_ref, x_ref, wa_ref, wb_ref, g_ref, b_ref = refs[:7]
    o_ref = refs[7 + ncast]
    _do_casts(refs[7:7 + ncast], refs[8 + ncast:])
    for r in range(o_ref.shape[0] // rb):
        rows = slice(r * rb, (r + 1) * rb)
        mix = _dot(og_ref[rows, :], wa_ref[...]) + _dot(op_ref[rows, :], wb_ref[...])
        o_ref[rows, :] = _layer_norm(alpha * x_ref[rows, :] + mix, g_ref[...], b_ref[...])


def _outproj(og, op, x, w, g, b, alpha, tm, rb, casts):
    m, d = x.shape
    ka, kb = og.shape[1], op.shape[1]
    assert ka % kb == 0 and w.shape[0] == ka + kb
    const = lambda i: (0, 0)
    once = pl.Buffered(1)
    cin, cout, cshape = _cast_specs(casts, m // tm, lambda i: i)
    return pl.pallas_call(
        functools.partial(_outproj_kernel, alpha=alpha, rb=rb, ncast=len(casts)),
        grid=(m // tm,),
        in_specs=[pl.BlockSpec((tm, ka), lambda i: (i, 0)),
                  pl.BlockSpec((tm, kb), lambda i: (i, 0)),
                  pl.BlockSpec((tm, d), lambda i: (i, 0)),
                  pl.BlockSpec((ka, d), const, pipeline_mode=once),
                  pl.BlockSpec((kb, d), lambda i: (ka // kb, 0), pipeline_mode=once),
                  pl.BlockSpec((1, d), const),
                  pl.BlockSpec((1, d), const)] + cin,
        out_specs=[pl.BlockSpec((tm, d), lambda i: (i, 0))] + cout,
        out_shape=[jax.ShapeDtypeStruct((m, d), F32)] + cshape,
        compiler_params=pltpu.CompilerParams(
            dimension_semantics=("parallel",), vmem_limit_bytes=VMEM_LIMIT),
        name="outproj_ln",
    )(og, op, x, w, w, g, b, *casts)


def _xattn_kernel(*refs, alpha, heads, rb, ncast):
    h_ref, wq_ref, k_ref, v_ref, wo_ref, g_ref, b_ref = refs[:7]
    o_ref = refs[7 + ncast]
    _do_casts(refs[7:7 + ncast], refs[8 + ncast:])
    d = h_ref.shape[-1]
    hd = d // heads
    for r in range(h_ref.shape[1] // rb):
        rows = slice(r * rb, (r + 1) * rb)
        h = h_ref[0, rows, :]
        q = _dot(h.astype(BF16), wq_ref[...])
        outs = []
        for i in range(heads):
            qh = q[:, i * hd:(i + 1) * hd].astype(BF16)
            kh = k_ref[0, :, i * hd:(i + 1) * hd]
            vh = v_ref[0, :, i * hd:(i + 1) * hd]
            s = lax.dot_general(qh, kh, _NT, preferred_element_type=F32) * (float(hd) ** -0.5)
            e = jnp.exp(s - jnp.max(s, axis=-1, keepdims=True))
            p = e / jnp.sum(e, axis=-1, keepdims=True)
            outs.append(_dot(p.astype(BF16), vh).astype(BF16))
        xa = _dot(jnp.concatenate(outs, axis=1), wo_ref[...])
        o_ref[0, rows, :] = _layer_norm(alpha * h + xa, g_ref[...], b_ref[...])


def _xattn(h, wq, k, v, wo, g, b, alpha, tm, rb, casts):
    bsz, t, d = h.shape
    mlen = k.shape[1]
    nj = t // tm
    const = lambda i, j: (0, 0)
    once = pl.Buffered(1)
    cin, cout, cshape = _cast_specs(casts, bsz * nj, lambda i, j: i * nj + j)
    return pl.pallas_call(
        functools.partial(_xattn_kernel, alpha=alpha, heads=XATTN_HEADS, rb=rb, ncast=len(casts)),
        grid=(bsz, nj),
        in_specs=[pl.BlockSpec((1, tm, d), lambda i, j: (i, j, 0)),
                  pl.BlockSpec((d, d), const, pipeline_mode=once),
                  pl.BlockSpec((1, mlen, d), lambda i, j: (i, 0, 0)),
                  pl.BlockSpec((1, mlen, d), lambda i, j: (i, 0, 0)),
                  pl.BlockSpec((d, d), const, pipeline_mode=once),
                  pl.BlockSpec((1, d), const),
                  pl.BlockSpec((1, d), const)] + cin,
        out_specs=[pl.BlockSpec((1, tm, d), lambda i, j: (i, j, 0))] + cout,
        out_shape=[jax.ShapeDtypeStruct((bsz, t, d), F32)] + cshape,
        compiler_params=pltpu.CompilerParams(
            dimension_semantics=("parallel", "parallel"), vmem_limit_bytes=VMEM_LIMIT),
        name="xattn_ln",
    )(h, wq, k, v, wo, g, b, *casts)


def _mlp_kernel(h_ref, wu_ref, wd_ref, g_ref, b_ref, o_ref, *, alpha, rb):
    j = pl.program_id(1)
    last = pl.num_programs(1) - 1

    def ff(rows):
        u = jnp.maximum(_dot(h_ref[rows, :].astype(BF16), wu_ref[...]), 0.0)
        return _dot((u * u).astype(BF16), wd_ref[...])

    @pl.when(j == 0)
    def _():
        o_ref[...] = ff(slice(None))

    @pl.when(jnp.logical_and(j > 0, j < last))
    def _():
        o_ref[...] += ff(slice(None))

    @pl.when(j == last)
    def _():
        for r in range(o_ref.shape[0] // rb):
            rows = slice(r * rb, (r + 1) * rb)
            y = alpha * h_ref[rows, :] + (o_ref[rows, :] + ff(rows))
            o_ref[rows, :] = _layer_norm(y, g_ref[...], b_ref[...])


def _mlp(h, wu, wd, g, b, alpha, tm, tf, rb):
    m, d = h.shape
    f = wu.shape[1]
    assert f // tf >= 2
    const = lambda i, j: (0, 0)
    return pl.pallas_call(
        functools.partial(_mlp_kernel, alpha=alpha, rb=rb),
        grid=(m // tm, f // tf),
        in_specs=[pl.BlockSpec((tm, d), lambda i, j: (i, 0)),
                  pl.BlockSpec((d, tf), lambda i, j: (0, j)),
                  pl.BlockSpec((tf, d), lambda i, j: (j, 0)),
                  pl.BlockSpec((1, d), const),
                  pl.BlockSpec((1, d), const)],
        out_specs=pl.BlockSpec((tm, d), lambda i, j: (i, 0)),
        out_shape=jax.ShapeDtypeStruct((m, d), F32),
        compiler_params=pltpu.CompilerParams(
            dimension_semantics=("parallel", "arbitrary"), vmem_limit_bytes=VMEM_LIMIT),
        name="mlp_ln",
    )(h, wu, wd, g, b)


def _pad_lanes(v):
    return jnp.pad(v.astype(F32), (0, LANE - v.shape[0]))[None, :]


def kernel(x, mem, w_in, conv_w, a_log, dt_bias, gdn_norm_w, pool_w, pool_scale, w_out, ln1_g, ln1_b,
           xq_w, xk_w, xv_w, xo_w, ln2_g, ln2_b, w_up, w_down, ln3_g, ln3_b):
    bsz, t, d = x.shape
    depth = w_in.shape[0]
    heads = a_log.shape[1]
    gw = conv_w.shape[2] // 3
    pw = pool_w.shape[1] * pool_w.shape[2]
    assert gw == heads * LANE and t % GDN_CHUNK == 0 and heads <= LANE
    alpha = (2.0 * depth) ** 0.25
    row2 = lambda v: v[None, :]

    h = x
    for l in range(depth):
        w16 = w_in[l].astype(BF16)
        zpad = jnp.zeros((d, LANE - heads), BF16)
        w_tail = jnp.concatenate(
            [w16[:, 4 * gw + 2 * heads:], w16[:, 4 * gw:4 * gw + heads], zpad,
             w16[:, 4 * gw + heads:4 * gw + 2 * heads], zpad], axis=1)
        assert pw % (2 * LANE) == 0

        proj, tail = _inproj(h.reshape(bsz * t, d), w16, 4 * gw, w_tail, tm=INPROJ_TM, tn=gw, sectioned=True)
        proj = proj.reshape(4, bsz, t, gw)
        tail = tail.reshape(bsz, t, pw + 2 * LANE)
        o_pool, beta, gc, wo16, xk16, xv16 = _pool(
            tail, pool_w[l].astype(BF16), row2(pool_scale[l]), 0,
            _pad_lanes(a_log[l]), _pad_lanes(dt_bias[l]), pw // (2 * LANE),
            casts=(w_out[l], xk_w[l], xv_w[l]))
        o_gdn = _gdn(proj, beta, gc, conv_w[l], row2(gdn_norm_w[l]), heads,
                     hb=GDN_HEADS_PER_STEP, cg=GDN_CHUNKS_PER_GROUP)

        h1, xq16, xo16 = _outproj(o_gdn.reshape(bsz * t, gw), o_pool.reshape(bsz * t, pw),
                                  h.reshape(bsz * t, d), wo16, row2(ln1_g[l]), row2(ln1_b[l]), alpha,
                                  tm=OUTPROJ_TM, rb=OUTPROJ_RB, casts=(xq_w[l], xo_w[l]))

        mlen = mem.shape[1]
        mk, mv = _inproj(mem.reshape(bsz * mlen, d), xk16, d, xv16, tm=INPROJ_TM, tn=KVPROJ_TN, out_dtype=BF16)
        h2, wu16, wd16 = _xattn(h1.reshape(bsz, t, d), xq16, mk.reshape(bsz, mlen, d), mv.reshape(bsz, mlen, d),
                                xo16, row2(ln2_g[l]), row2(ln2_b[l]), alpha, tm=XATTN_TM, rb=XATTN_RB,
                                casts=(w_up[l], w_down[l]))

        h3 = _mlp(h2.reshape(bsz * t, d), wu16, wd16,
                  row2(ln3_g[l]), row2(ln3_b[l]), alpha, tm=MLP_TM, tf=MLP_TF, rb=MLP_RB)
        h = h3.reshape(bsz, t, d)
    return h
```
